```python
import math
import jax, jax.numpy as jnp
from jax import lax
import numpy as np

D_MODEL = 2048
BATCH = 2
SEQ = 8192
DEPTH = 1

RET_HEADS = 8
RET_DK = 128
RET_DV = 256
RET_CHUNK = 128
POOL_WINDOWS = (2, 4, 8, 16)
POOL_GROUPS = len(POOL_WINDOWS)
POOL_DIM = 1024
POOL_GDIM = POOL_DIM // POOL_GROUPS
N_BRANCH = 2
PEER_HEADS = 8
PEER_NKEYS = 128
PEER_N_EXPERTS = PEER_NKEYS * PEER_NKEYS
PEER_DKEY = 256
PEER_DHALF = PEER_DKEY // 2
PEER_TOPK = 16
PEER_BLOCK = 128

ROPE_BASE = 10000.0
EPS = 1e-6

Q_DIM = RET_HEADS * RET_DK
V_DIM = RET_HEADS * RET_DV
SPLITS = (Q_DIM, Q_DIM, V_DIM, V_DIM, POOL_DIM, N_BRANCH * D_MODEL)
IN_COLS = sum(SPLITS)

kernel_name = "hybrid_retention_pool_peer_block"


def rmsnorm(x, w):
    xf = x.astype(jnp.float32)
    y = xf * lax.rsqrt(jnp.mean(xf * xf, axis=-1, keepdims=True) + EPS)
    return (y * w.astype(jnp.float32)).astype(x.dtype)


def rotary(x):
    S, d = x.shape[1], x.shape[-1]
    half = d // 2
    inv_freq = ROPE_BASE ** (-jnp.arange(half, dtype=jnp.float32) / half)
    ang = jnp.arange(S, dtype=jnp.float32)[:, None] * inv_freq[None, :]
    cos = jnp.cos(ang)[None, :, None, :]
    sin = jnp.sin(ang)[None, :, None, :]
    xf = x.astype(jnp.float32)
    x1, x2 = xf[..., :half], xf[..., half:]
    out = jnp.concatenate([x1 * cos - x2 * sin, x1 * sin + x2 * cos], axis=-1)
    return out.astype(x.dtype)


def retention_chunkwise(q, k, v):
    B, S, H, dk = q.shape
    dv = v.shape[-1]
    C = RET_CHUNK
    N = S // C
    log_g = jnp.log(1.0 - 2.0 ** (-5.0 - jnp.arange(H, dtype=jnp.float32)))
    i = jnp.arange(C, dtype=jnp.float32)
    diff = i[:, None] - i[None, :]
    dmask = jnp.where(diff[None] >= 0,
                      jnp.exp(log_g[:, None, None] * jnp.maximum(diff, 0.0)[None]), 0.0)
    xi = jnp.exp(log_g[None, :] * (i[:, None] + 1.0))
    zeta = jnp.exp(log_g[None, :] * (C - 1.0 - i)[:, None])
    g_chunk = jnp.exp(log_g * C)

    qc = q.reshape(B, N, C, H, dk).astype(jnp.float32)
    kc = k.reshape(B, N, C, H, dk).astype(jnp.float32)
    vc = v.reshape(B, N, C, H, dv).astype(jnp.float32)

    scores = jnp.einsum('bnihd,bnjhd->bnhij', qc, kc) * dmask[None, None]
    inner = jnp.einsum('bnhij,bnjhe->bnihe', scores, vc)

    qx = jnp.moveaxis(qc * xi[None, None, :, :, None], 1, 0)
    kz = jnp.moveaxis(kc * zeta[None, None, :, :, None], 1, 0)
    vn = jnp.moveaxis(vc, 1, 0)

    def step(R, inp):
        qi, ki, vi = inp
        cross = jnp.einsum('bihd,bhde->bihe', qi, R)
        R = g_chunk[None, :, None, None] * R + jnp.einsum('bjhd,bjhe->bhde', ki, vi)
        return R, cross

    R0 = jnp.zeros((B, H, dk, dv), jnp.float32)
    _, cross = lax.scan(step, R0, (qx, kz, vn))
    out = inner + jnp.moveaxis(cross, 0, 1)
    return out.reshape(B, S, H, dv)


def head_groupnorm(y):
    mu = jnp.mean(y, axis=-1, keepdims=True)
    var = jnp.mean((y - mu) ** 2, axis=-1, keepdims=True)
    return (y - mu) * lax.rsqrt(var + EPS)


def multiscale_pool(p):
    B, S, _ = p.shape
    pf = p.astype(jnp.float32)
    cs = jnp.cumsum(pf, axis=1)
    pos = jnp.arange(S, dtype=jnp.int32)
    outs = []
    for g, w in enumerate(POOL_WINDOWS):
        sl = slice(g * POOL_GDIM, (g + 1) * POOL_GDIM)
        csg = cs[..., sl]
        shifted = jnp.pad(csg, ((0, 0), (w, 0), (0, 0)))[:, :S]
        count = jnp.minimum(pos + 1, w).astype(jnp.float32)
        mean = (csg - shifted) / count[None, :, None]
        outs.append(mean - pf[..., sl])
    return jnp.stack(outs, axis=2).astype(p.dtype)


def peer_ffn(h, w_q, sub_keys, u, v):
    B, S, D = h.shape
    T = B * S
    xs = h.reshape(T // PEER_BLOCK, PEER_BLOCK, D)

    def block(xb):
        q = (xb @ w_q).reshape(PEER_BLOCK, PEER_HEADS, 2, PEER_DHALF)
        s = jnp.einsum('thpd,pkd->thpk', q, sub_keys).astype(jnp.float32)
        sv, si = lax.top_k(s, PEER_TOPK)
        cand = (sv[:, :, 0, :, None] + sv[:, :, 1, None, :]).reshape(PEER_BLOCK, PEER_HEADS, -1)
        cidx = (si[:, :, 0, :, None] * PEER_NKEYS + si[:, :, 1, None, :]).reshape(PEER_BLOCK, PEER_HEADS, -1)
        best, pos = lax.top_k(cand, PEER_TOPK)
        eidx = jnp.take_along_axis(cidx, pos, axis=-1)
        gate = jax.nn.softmax(best, axis=-1)
        u_sel = u[eidx]
        v_sel = v[eidx]
        a = jnp.einsum('thkd,td->thk', u_sel, xb).astype(jnp.float32)
        wgt = (jax.nn.gelu(a, approximate=False) * gate).astype(xb.dtype)
        return jnp.einsum('thk,thkd->td', wgt, v_sel)

    y = lax.map(block, xs)
    return y.reshape(B, S, D)


def setup_inputs(seed: int = 0) -> dict:
    key = jax.random.key(seed)
    ks = jax.random.split(key, 16)
    f32 = jnp.float32
    nrm = lambda k, shape, scale: jax.random.normal(k, shape, f32) * scale
    return {
        "x": nrm(ks[0], (BATCH, SEQ, D_MODEL), 1.0),
        "norm1_w": 1.0 + nrm(ks[1], (DEPTH, D_MODEL), 0.02),
        "w_in": nrm(ks[2], (DEPTH, D_MODEL, IN_COLS), D_MODEL ** -0.5),
        "w_ret_o": nrm(ks[3], (DEPTH, V_DIM, D_MODEL), V_DIM ** -0.5),
        "w_pool_lin": nrm(ks[4], (DEPTH, POOL_GROUPS, POOL_GDIM, POOL_GDIM), POOL_GDIM ** -0.5),
        "pool_scale": 1.0 + nrm(ks[5], (DEPTH, POOL_DIM), 0.1),
        "w_pool_o": nrm(ks[6], (DEPTH, POOL_DIM, D_MODEL), POOL_DIM ** -0.5),
        "w_out": nrm(ks[7], (DEPTH, D_MODEL, D_MODEL), D_MODEL ** -0.5),
        "norm2_w": 1.0 + nrm(ks[8], (DEPTH, D_MODEL), 0.02),
        "peer_w_q": nrm(ks[9], (DEPTH, D_MODEL, PEER_HEADS * PEER_DKEY), D_MODEL ** -0.5),
        "peer_sub_keys": nrm(ks[10], (DEPTH, 2, PEER_NKEYS, PEER_DHALF), PEER_DHALF ** -0.5),
        "peer_u": nrm(ks[11], (DEPTH, PEER_N_EXPERTS, D_MODEL), D_MODEL ** -0.5),
        "peer_v": nrm(ks[12], (DEPTH, PEER_N_EXPERTS, D_MODEL), 0.5),
        "final_norm_w": 1.0 + nrm(ks[13], (D_MODEL,), 0.02),
    }


def reference(x, norm1_w, w_in, w_ret_o, w_pool_lin, pool_scale, w_pool_o, w_out,
              norm2_w, peer_w_q, peer_sub_keys, peer_u, peer_v, final_norm_w):
    B, S, D = x.shape
    offsets = np.cumsum(SPLITS)[:-1].tolist()
    for l in range(DEPTH):
        h = rmsnorm(x, norm1_w[l])
        proj = h @ w_in[l]
        q, k, vr, gr, pin, gates = jnp.split(proj, offsets, axis=-1)
        q = rotary(q.reshape(B, S, RET_HEADS, RET_DK))
        k = rotary(k.reshape(B, S, RET_HEADS, RET_DK)) * (RET_DK ** -0.5)
        vr = vr.reshape(B, S, RET_HEADS, RET_DV)
        ret = head_groupnorm(retention_chunkwise(q, k, vr)).reshape(B, S, V_DIM)
        ret = (jax.nn.silu(gr.astype(jnp.float32)) * ret).astype(x.dtype)
        y_ret = ret @ w_ret_o[l]

        pooled = multiscale_pool(pin)
        pooled = jnp.einsum('bsgc,gcd->bsgd', pooled, w_pool_lin[l]).reshape(B, S, POOL_DIM)
        y_pool = (pooled * pool_scale[l]) @ w_pool_o[l]

        g_ret, g_pool = jnp.split(jax.nn.sigmoid(gates.astype(jnp.float32)), 2, axis=-1)
        mixed = (g_ret * y_ret + g_pool * y_pool).astype(x.dtype)
        x = x + mixed @ w_out[l]
        h2 = rmsnorm(x, norm2_w[l])
        x = x + peer_ffn(h2, peer_w_q[l], peer_sub_keys[l], peer_u[l], peer_v[l])
    return rmsnorm(x, final_norm_w)
```

```python
import functools
import math

import numpy as np
import jax
import jax.numpy as jnp
from jax import lax
from jax.experimental import pallas as pl
from jax.experimental.pallas import tpu as pltpu

F32 = jnp.float32
BF16 = jnp.bfloat16
I32 = jnp.int32

D_MODEL = 2048
RET_HEADS = 8
RET_DK = 128
RET_DV = 256
RET_CHUNK = 128
POOL_WINDOWS = (2, 4, 8, 16)
POOL_GROUPS = len(POOL_WINDOWS)
POOL_DIM = 1024
POOL_GDIM = POOL_DIM // POOL_GROUPS
POOL_HALO = 16
PEER_HEADS = 8
PEER_NKEYS = 128
PEER_DKEY = 256
PEER_DHALF = PEER_DKEY // 2
PEER_TOPK = 16
PEER_SEL = PEER_HEADS * PEER_TOPK
ROPE_BASE = 10000.0
EPS = 1e-6

Q_DIM = RET_HEADS * RET_DK
V_DIM = RET_HEADS * RET_DV
OFF_Q = 0
OFF_K = OFF_Q + Q_DIM
OFF_V = OFF_K + Q_DIM
OFF_G = OFF_V + V_DIM
OFF_P = OFF_G + V_DIM
OFF_BG = OFF_P + POOL_DIM
IN_COLS = OFF_BG + 2 * D_MODEL

LANES = 128
ROW_SUB = D_MODEL // LANES
VMEM_LIMIT = 48 * 1024 * 1024


def _cparams(n_axes):
    return pltpu.CompilerParams(
        dimension_semantics=("arbitrary",) * n_axes, vmem_limit_bytes=VMEM_LIMIT)


def _inproj_kernel(x_ref, nw_ref, w_ref, o_ref, h_scr):
    @pl.when(pl.program_id(1) == 0)
    def _():
        x = x_ref[...]
        ms = jnp.mean(x * x, axis=-1, keepdims=True)
        h_scr[...] = ((x * lax.rsqrt(ms + EPS)) * nw_ref[...]).astype(BF16)

    o_ref[...] = jnp.dot(h_scr[...], w_ref[...], preferred_element_type=F32)


def _inproj(x2, norm_w, w_bf):
    T = x2.shape[0]
    tm, tn = 512, 1024
    return pl.pallas_call(
        _inproj_kernel,
        grid=(T // tm, IN_COLS // tn),
        in_specs=[
            pl.BlockSpec((tm, D_MODEL), lambda i, j: (i, 0)),
            pl.BlockSpec((1, D_MODEL), lambda i, j: (0, 0)),
            pl.BlockSpec((D_MODEL, tn), lambda i, j: (0, j)),
        ],
        out_specs=pl.BlockSpec((tm, tn), lambda i, j: (i, j)),
        out_shape=jax.ShapeDtypeStruct((T, IN_COLS), F32),
        scratch_shapes=[pltpu.VMEM((tm, D_MODEL), BF16)],
        compiler_params=_cparams(2),
        name="inproj",
    )(x2, norm_w.reshape(1, D_MODEL), w_bf)


def _retention_constants(seq):
    H, C = RET_HEADS, RET_CHUNK
    half = RET_DK // 2
    inv_freq = ROPE_BASE ** (-jnp.arange(half, dtype=F32) / half)
    ang = jnp.arange(seq, dtype=F32)[:, None] * inv_freq[None, :]
    cos, sin = jnp.cos(ang), jnp.sin(ang)
    cos_full = jnp.concatenate([cos, cos], axis=-1)
    sin_signed = jnp.concatenate([-sin, sin], axis=-1)
    log_g = jnp.log(1.0 - 2.0 ** (-5.0 - jnp.arange(H, dtype=F32)))
    i = jnp.arange(C, dtype=F32)
    diff = i[:, None] - i[None, :]
    dmask = jnp.where(diff[None] >= 0,
                      jnp.exp(log_g[:, None, None] * jnp.maximum(diff, 0.0)[None]), 0.0)
    xi = jnp.exp(log_g[:, None] * (i[None, :] + 1.0))
    zeta = jnp.exp(log_g[:, None] * (C - 1.0 - i)[None, :])
    g_chunk = jnp.exp(log_g * C)
    xi_b = jnp.broadcast_to(xi[:, :, None], (H, C, RET_DK))
    zeta_b = jnp.broadcast_to(zeta[:, :, None], (H, C, RET_DK))
    g_b = jnp.broadcast_to(g_chunk[:, None, None], (H, 8, RET_DV))
    return cos_full, sin_signed, dmask, xi_b, zeta_b, g_b


def _ret_kernel(q_ref, k_ref, v_ref, g_ref, cos_ref, sin_ref, dm_ref, xi_ref, ze_ref, gc_ref,
                o_ref, r_scr, *, n_chunks):
    @pl.when(pl.program_id(1) == 0)
    def _():
        r_scr[...] = jnp.zeros_like(r_scr)

    scale = RET_DK ** -0.5
    for c in range(n_chunks):
        rows = slice(c * RET_CHUNK, (c + 1) * RET_CHUNK)
        cos = cos_ref[rows, :]
        sin = sin_ref[rows, :]
        for h in range(RET_HEADS):
            kc = slice(h * RET_DK, (h + 1) * RET_DK)
            vc = slice(h * RET_DV, (h + 1) * RET_DV)
            q = q_ref[0, rows, kc]
            k = k_ref[0, rows, kc]
            qr = q * cos + pltpu.roll(q, RET_DK // 2, 1) * sin
            kr = (k * cos + pltpu.roll(k, RET_DK // 2, 1) * sin) * scale
            vb = v_ref[0, rows, vc].astype(BF16)
            sc = lax.dot_general(qr.astype(BF16), kr.astype(BF16), (((1,), (1,)), ((), ())),
                                 preferred_element_type=F32) * dm_ref[h]
            inner = jnp.dot(sc.astype(BF16), vb, preferred_element_type=F32)
            state = r_scr[h]
            cross = jnp.dot((qr * xi_ref[h]).astype(BF16), state.astype(BF16),
                            preferred_element_type=F32)
            kzt = (kr * ze_ref[h]).T.astype(BF16)
            r_scr[h] = gc_ref[h, 0:1, :] * state + jnp.dot(kzt, vb, preferred_element_type=F32)
            o = inner + cross
            mu = jnp.mean(o, axis=-1, keepdims=True)
            var = jnp.mean((o - mu) ** 2, axis=-1, keepdims=True)
            y = (o - mu) * lax.rsqrt(var + EPS)
            g = g_ref[0, rows, vc]
            o_ref[0, rows, vc] = ((g * jax.nn.sigmoid(g)) * y).astype(BF16)


def _retention(proj3, consts):
    B, S, _ = proj3.shape
    n_chunks = 2
    tm = n_chunks * RET_CHUNK
    cos_full, sin_signed, dmask, xi_b, zeta_b, g_b = consts
    H, C = RET_HEADS, RET_CHUNK
    full3 = lambda b, j: (0, 0, 0)
    return pl.pallas_call(
        functools.partial(_ret_kernel, n_chunks=n_chunks),
        grid=(B, S // tm),
        in_specs=[
            pl.BlockSpec((1, tm, Q_DIM), lambda b, j: (b, j, OFF_Q // Q_DIM)),
            pl.BlockSpec((1, tm, Q_DIM), lambda b, j: (b, j, OFF_K // Q_DIM)),
            pl.BlockSpec((1, tm, V_DIM), lambda b, j: (b, j, OFF_V // V_DIM)),
            pl.BlockSpec((1, tm, V_DIM), lambda b, j: (b, j, OFF_G // V_DIM)),
            pl.BlockSpec((tm, RET_DK), lambda b, j: (j, 0)),
            pl.BlockSpec((tm, RET_DK), lambda b, j: (j, 0)),
            pl.BlockSpec((H, C, C), full3),
            pl.BlockSpec((H, C, RET_DK), full3),
            pl.BlockSpec((H, C, RET_DK), full3),
            pl.BlockSpec((H, 8, RET_DV), full3),
        ],
        out_specs=pl.BlockSpec((1, tm, V_DIM), lambda b, j: (b, j, 0)),
        out_shape=jax.ShapeDtypeStruct((B, S, V_DIM), BF16),
        scratch_shapes=[pltpu.VMEM((H, RET_DK, RET_DV), F32)],
        compiler_params=_cparams(2),
        name="retention",
    )(proj3, proj3, proj3, proj3, cos_full, sin_signed, dmask, xi_b, zeta_b, g_b)


def _pool_kernel(cur_ref, halo_ref, wl_ref, sc_ref, o_ref, *, tm):
    j = pl.program_id(1)
    cur = cur_ref[0]
    halo = jnp.where(j == 0, 0.0, halo_ref[0])
    ext = jnp.concatenate([halo, cur], axis=0)
    pos = j * tm + lax.broadcasted_iota(I32, (tm, 1), 0)
    for g, w in enumerate(POOL_WINDOWS):
        cols = slice(g * POOL_GDIM, (g + 1) * POOL_GDIM)
        s = ext[:, cols]
        sh = 1
        while sh < w:
            s = s + pltpu.roll(s, sh, 0)
            sh *= 2
        win = s[POOL_HALO:, :]
        count = jnp.minimum(pos + 1, w).astype(F32)
        pooled = win / count - cur[:, cols]
        lin = jnp.dot(pooled.astype(BF16), wl_ref[g], preferred_element_type=F32)
        o_ref[0, :, cols] = (lin * sc_ref[:, cols]).astype(BF16)


def _pool(proj3, wl_bf, pool_scale):
    B, S, _ = proj3.shape
    tm = 512
    return pl.pallas_call(
        functools.partial(_pool_kernel, tm=tm),
        grid=(B, S // tm),
        in_specs=[
            pl.BlockSpec((1, tm, POOL_DIM), lambda b, j: (b, j, OFF_P // POOL_DIM)),
            pl.BlockSpec((1, POOL_HALO, POOL_DIM),
                         lambda b, j: (b, jnp.maximum(j * (tm // POOL_HALO) - 1, 0), OFF_P // POOL_DIM)),
            pl.BlockSpec((POOL_GROUPS, POOL_GDIM, POOL_GDIM), lambda b, j: (0, 0, 0)),
            pl.BlockSpec((1, POOL_DIM), lambda b, j: (0, 0)),
        ],
        out_specs=pl.BlockSpec((1, tm, POOL_DIM), lambda b, j: (b, j, 0)),
        out_shape=jax.ShapeDtypeStruct((B, S, POOL_DIM), BF16),
        compiler_params=_cparams(2),
        name="pool",
    )(proj3, proj3, wl_bf, pool_scale.reshape(1, POOL_DIM))


def _mix_kernel(ret_ref, wr_ref, pool_ref, wp_ref, gr_ref, gp_ref, o_ref):
    y_ret = jnp.dot(ret_ref[...], wr_ref[...], preferred_element_type=F32)
    y_pool = jnp.dot(pool_ref[...], wp_ref[...], preferred_element_type=F32)
    mixed = jax.nn.sigmoid(gr_ref[...]) * y_ret + jax.nn.sigmoid(gp_ref[...]) * y_pool
    o_ref[...] = mixed.astype(BF16)


def _mix(ret2, wr_bf, pool2, wp_bf, proj2):
    T = ret2.shape[0]
    tm, tn = 512, 512
    off_r = OFF_BG // tn
    off_p = (OFF_BG + D_MODEL) // tn
    return pl.pallas_call(
        _mix_kernel,
        grid=(T // tm, D_MODEL // tn),
        in_specs=[
            pl.BlockSpec((tm, V_DIM), lambda i, j: (i, 0)),
            pl.BlockSpec((V_DIM, tn), lambda i, j: (0, j)),
            pl.BlockSpec((tm, POOL_DIM), lambda i, j: (i, 0)),
            pl.BlockSpec((POOL_DIM, tn), lambda i, j: (0, j)),
            pl.BlockSpec((tm, tn), lambda i, j: (i, off_r + j)),
            pl.BlockSpec((tm, tn), lambda i, j: (i, off_p + j)),
        ],
        out_specs=pl.BlockSpec((tm, tn), lambda i, j: (i, j)),
        out_shape=jax.ShapeDtypeStruct((T, D_MODEL), BF16),
        compiler_params=_cparams(2),
        name="mix",
    )(ret2, wr_bf, pool2, wp_bf, proj2, proj2)


def _resid_kernel(m_ref, w_ref, x_ref, o_ref):
    o_ref[...] = x_ref[...] + jnp.dot(m_ref[...], w_ref[...], preferred_element_type=F32)


def _resid(mixed, w_bf, x2):
    T = x2.shape[0]
    tm, tn = 512, 512
    return pl.pallas_call(
        _resid_kernel,
        grid=(T // tm, D_MODEL // tn),
        in_specs=[
            pl.BlockSpec((tm, D_MODEL), lambda i, j: (i, 0)),
            pl.BlockSpec((D_MODEL, tn), lambda i, j: (0, j)),
            pl.BlockSpec((tm, tn), lambda i, j: (i, j)),
        ],
        out_specs=pl.BlockSpec((tm, tn), lambda i, j: (i, j)),
        out_shape=jax.ShapeDtypeStruct((T, D_MODEL), F32),
        compiler_params=_cparams(2),
        name="resid",
    )(mixed, w_bf, x2)


def _topk_axis0(s, k, iota, n):
    vals, idxs = [], []
    for _ in range(k):
        m = jnp.max(s, axis=0, keepdims=True)
        idx = jnp.min(jnp.where(s == m, iota, n), axis=0, keepdims=True)
        vals.append(m)
        idxs.append(idx)
        s = jnp.where(iota == idx, -jnp.inf, s)
    return jnp.concatenate(vals, axis=0), jnp.concatenate(idxs, axis=0)


def _peerq_kernel(x_ref, nw_ref, wq_ref, keys_ref, h_ref, e_ref, g_ref, q_scr, *, tm):
    x = x_ref[...]
    ms = jnp.mean(x * x, axis=-1, keepdims=True)
    hb = ((x * lax.rsqrt(ms + EPS)) * nw_ref[...]).astype(BF16)
    h_ref[...] = hb.astype(F32)
    q_scr[...] = jnp.dot(hb, wq_ref[...], preferred_element_type=F32).astype(BF16)

    K = PEER_TOPK
    iota_k = lax.broadcasted_iota(I32, (PEER_NKEYS, tm), 0)
    iota_c = lax.broadcasted_iota(I32, (K * K, tm), 0)

    def head(h, carry):
        sv, si = [], []
        for p in range(2):
            off = pl.multiple_of(h * PEER_DKEY + p * PEER_DHALF, PEER_DHALF)
            qhp = q_scr[:, pl.ds(off, PEER_DHALF)]
            s = lax.dot_general(keys_ref[p], qhp, (((1,), (1,)), ((), ())),
                                preferred_element_type=F32)
            v, i = _topk_axis0(s, K, iota_k, PEER_NKEYS)
            sv.append(v)
            si.append(i)
        cand = (sv[0][:, None, :] + sv[1][None, :, :]).reshape(K * K, tm)
        cidx = (si[0][:, None, :] * PEER_NKEYS + si[1][None, :, :]).reshape(K * K, tm)
        best, eidx = [], []
        for _ in range(K):
            m = jnp.max(cand, axis=0, keepdims=True)
            pos = jnp.min(jnp.where(cand == m, iota_c, K * K), axis=0, keepdims=True)
            hit = iota_c == pos
            best.append(m)
            eidx.append(jnp.max(jnp.where(hit, cidx, -1), axis=0, keepdims=True))
            cand = jnp.where(hit, -jnp.inf, cand)
        best = jnp.concatenate(best, axis=0)
        ex = jnp.exp(best - best[0:1, :])
        gate = ex / jnp.sum(ex, axis=0, keepdims=True)
        row = pl.multiple_of(h * K, K)
        e_ref[pl.ds(row, K), :] = jnp.concatenate(eidx, axis=0)
        g_ref[pl.ds(row, K), :] = gate
        return carry

    lax.fori_loop(0, PEER_HEADS, head, 0)


def _peerq(x1, norm_w, wq_bf, keys_bf):
    T = x1.shape[0]
    tm = 256
    return pl.pallas_call(
        functools.partial(_peerq_kernel, tm=tm),
        grid=(T // tm,),
        in_specs=[
            pl.BlockSpec((tm, D_MODEL), lambda i: (i, 0)),
            pl.BlockSpec((1, D_MODEL), lambda i: (0, 0)),
            pl.BlockSpec((D_MODEL, PEER_HEADS * PEER_DKEY), lambda i: (0, 0)),
            pl.BlockSpec((2, PEER_NKEYS, PEER_DHALF), lambda i: (0, 0, 0)),
        ],
        out_specs=[
            pl.BlockSpec((tm, D_MODEL), lambda i: (i, 0)),
            pl.BlockSpec((PEER_SEL, tm), lambda i: (0, i)),
            pl.BlockSpec((PEER_SEL, tm), lambda i: (0, i)),
        ],
        out_shape=[
            jax.ShapeDtypeStruct((T, D_MODEL), F32),
            jax.ShapeDtypeStruct((PEER_SEL, T), I32),
            jax.ShapeDtypeStruct((PEER_SEL, T), F32),
        ],
        scratch_shapes=[pltpu.VMEM((tm, PEER_HEADS * PEER_DKEY), BF16)],
        compiler_params=_cparams(1),
        name="peerq",
    )(x1, norm_w.reshape(1, D_MODEL), wq_bf, keys_bf)


PEER_SLOTS = 4
PEER_TB = 128


def _peer_kernel(idx_ref, gate_ref, h_ref, tab_ref, y_ref, buf, wbc, sem):
    def row_copy(tok, r, slot):
        return pltpu.make_async_copy(tab_ref.at[idx_ref[tok, r]], buf.at[slot, r], sem.at[slot])

    def issue(tok, slot):
        for r in range(PEER_SEL):
            row_copy(tok, r, slot).start()

    def wait(slot):
        pltpu.make_async_copy(tab_ref.at[pl.ds(0, PEER_SEL)], buf.at[slot], sem.at[slot]).wait()

    def compute(tok, slot):
        h = h_ref[tok]
        parts = []
        for g in range(PEER_SEL // 8):
            w8 = buf[slot, g * 8:(g + 1) * 8]
            u8 = lax.bitcast_convert_type(w8 & jnp.int32(-65536), F32)
            p8 = u8 * h[None]
            p8 = p8[:, :8, :] + p8[:, 8:, :]
            parts.append(jnp.sum(p8, axis=1))
        a_col = jnp.sum(jnp.concatenate(parts, axis=0), axis=1, keepdims=True)
        a_row = jnp.broadcast_to(a_col, (PEER_SEL, LANES)).T[0:1, :]
        act = 0.5 * a_row * (1.0 + lax.erf(a_row * np.float32(math.sqrt(0.5))))
        w_row = (act * gate_ref[pl.ds(tok, 1), :]).astype(BF16).astype(F32)
        wbc[...] = jnp.broadcast_to(w_row, (PEER_SEL, LANES)).T
        accs = [jnp.zeros((ROW_SUB, LANES), F32) for _ in range(4)]
        for r in range(PEER_SEL):
            v = lax.bitcast_convert_type(buf[slot, r] << 16, F32)
            accs[r % 4] = accs[r % 4] + wbc[r:r + 1, :] * v
        y_ref[tok] = (accs[0] + accs[1]) + (accs[2] + accs[3])

    for t in range(PEER_SLOTS - 1):
        issue(t, t)

    def body(t, carry):
        issue(t + (PEER_SLOTS - 1), (t + (PEER_SLOTS - 1)) % PEER_SLOTS)
        slot = t % PEER_SLOTS
        wait(slot)
        compute(t, slot)
        return carry

    n_main = PEER_TB - (PEER_SLOTS - 1)
    lax.fori_loop(0, n_main, body, 0)
    for t in range(n_main, PEER_TB):
        wait(t % PEER_SLOTS)
        compute(t, t % PEER_SLOTS)


def _peer(eidx, gate, h3, table):
    T = eidx.shape[0]
    tb = PEER_TB
    return pl.pallas_call(
        _peer_kernel,
        grid=(T // tb,),
        in_specs=[
            pl.BlockSpec((tb, PEER_SEL), lambda i: (i, 0), memory_space=pltpu.SMEM),
            pl.BlockSpec((tb, PEER_SEL), lambda i: (i, 0)),
            pl.BlockSpec((tb, ROW_SUB, LANES), lambda i: (i, 0, 0)),
            pl.BlockSpec(memory_space=pl.ANY),
        ],
        out_specs=pl.BlockSpec((tb, ROW_SUB, LANES), lambda i: (i, 0, 0)),
        out_shape=jax.ShapeDtypeStruct((T, ROW_SUB, LANES), F32),
        scratch_shapes=[
            pltpu.VMEM((PEER_SLOTS, PEER_SEL, ROW_SUB, LANES), I32),
            pltpu.VMEM((PEER_SEL, LANES), F32),
            pltpu.SemaphoreType.DMA((PEER_SLOTS,)),
        ],
        compiler_params=_cparams(1),
        name="peer",
    )(eidx, gate, h3, table)


def _pack_table(u, v):
    ub = lax.bitcast_convert_type(u.astype(BF16), jnp.uint16).astype(jnp.uint32)
    vb = lax.bitcast_convert_type(v.astype(BF16), jnp.uint16).astype(jnp.uint32)
    words = lax.bitcast_convert_type((ub << 16) | vb, I32)
    return words.reshape(u.shape[0], ROW_SUB, LANES)


def _final_kernel(x_ref, y_ref, nw_ref, o_ref):
    x = x_ref[...] + y_ref[...]
    ms = jnp.mean(x * x, axis=-1, keepdims=True)
    o_ref[...] = (x * lax.rsqrt(ms + EPS)) * nw_ref[...]


def _final(x1, y, norm_w):
    T = x1.shape[0]
    tm = 512
    return pl.pallas_call(
        _final_kernel,
        grid=(T // tm,),
        in_specs=[
            pl.BlockSpec((tm, D_MODEL), lambda i: (i, 0)),
            pl.BlockSpec((tm, D_MODEL), lambda i: (i, 0)),
            pl.BlockSpec((1, D_MODEL), lambda i: (0, 0)),
        ],
        out_specs=pl.BlockSpec((tm, D_MODEL), lambda i: (i, 0)),
        out_shape=jax.ShapeDtypeStruct((T, D_MODEL), F32),
        compiler_params=_cparams(1),
        name="final",
    )(x1, y, norm_w.reshape(1, D_MODEL))


def kernel(x, norm1_w, w_in, w_ret_o, w_pool_lin, pool_scale, w_pool_o, w_out,
           norm2_w, peer_w_q, peer_sub_keys, peer_u, peer_v, final_norm_w):
    B, S, D = x.shape
    T = B * S
    depth = w_in.shape[0]
    consts = _retention_constants(S)
    x2 = x.reshape(T, D)
    for l in range(depth):
        proj = _inproj(x2, norm1_w[l], w_in[l].astype(BF16))
        proj3 = proj.reshape(B, S, IN_COLS)
        ret = _retention(proj3, consts)
        pooled = _pool(proj3, w_pool_lin[l].astype(BF16), pool_scale[l])
        mixed = _mix(ret.reshape(T, V_DIM), w_ret_o[l].astype(BF16),
                     pooled.reshape(T, POOL_DIM), w_pool_o[l].astype(BF16), proj)
        x1 = _resid(mixed, w_out[l].astype(BF16), x2)
        h2, eidx_t, gate_t = _peerq(x1, norm2_w[l], peer_w_q[l].astype(BF16),
                                    peer_sub_keys[l].astype(BF16))
        table = _pack_table(peer_u[l], peer_v[l])
        y = _peer(eidx_t.T, gate_t.T, h2.reshape(T, ROW_SUB, LANES), table)
        x2 = x1 + y.reshape(T, D) if l + 1 < depth else x1
        y_last = y.reshape(T, D)
    out = _final(x2, y_last, final_norm_w)
    return out.reshape(B, S, D)
```

```python
import functools
import math

import numpy as np
import jax
import jax.numpy as jnp
from jax import lax
from jax.experimental import pallas as pl
from jax.experimental.pallas import tpu as pltpu

F32 = jnp.float32
BF16 = jnp.bfloat16
I32 = jnp.int32

D_MODEL = 2048
RET_HEADS = 8
RET_DK = 128
RET_DV = 256
RET_CHUNK = 128
POOL_WINDOWS = (2, 4, 8, 16)
POOL_GROUPS = len(POOL_WINDOWS)
POOL_DIM = 1024
POOL_GDIM = POOL_DIM // POOL_GROUPS
POOL_HALO = 16
PEER_HEADS = 8
PEER_NKEYS = 128
PEER_DKEY = 256
PEER_DHALF = PEER_DKEY // 2
PEER_TOPK = 16
PEER_SEL = PEER_HEADS * PEER_TOPK
ROPE_BASE = 10000.0
EPS = 1e-6

Q_DIM = RET_HEADS * RET_DK
V_DIM = RET_HEADS * RET_DV
OFF_Q = 0
OFF_K = OFF_Q + Q_DIM
OFF_V = OFF_K + Q_DIM
OFF_G = OFF_V + V_DIM
OFF_P = OFF_G + V_DIM
OFF_BG = OFF_P + POOL_DIM
IN_COLS = OFF_BG + 2 * D_MODEL

LANES = 128
ROW_SUB = D_MODEL // LANES
VMEM_LIMIT = 48 * 1024 * 1024


def _cparams(n_axes):
    return pltpu.CompilerParams(
        dimension_semantics=("arbitrary",) * n_axes, vmem_limit_bytes=VMEM_LIMIT)


def _inproj_kernel(x_ref, nw_ref, w_ref, o_ref, h_scr):
    @pl.when(pl.program_id(1) == 0)
    def _():
        x = x_ref[...]
        ms = jnp.mean(x * x, axis=-1, keepdims=True)
        h_scr[...] = ((x * lax.rsqrt(ms + EPS)) * nw_ref[...]).astype(BF16)

    o_ref[...] = jnp.dot(h_scr[...], w_ref[...], preferred_element_type=F32)


def _inproj(x2, norm_w, w_bf):
    T = x2.shape[0]
    tm, tn = 512, 1024
    return pl.pallas_call(
        _inproj_kernel,
        grid=(T // tm, IN_COLS // tn),
        in_specs=[
            pl.BlockSpec((tm, D_MODEL), lambda i, j: (i, 0)),
            pl.BlockSpec((1, D_MODEL), lambda i, j: (0, 0)),
            pl.BlockSpec((D_MODEL, tn), lambda i, j: (0, j)),
        ],
        out_specs=pl.BlockSpec((tm, tn), lambda i, j: (i, j)),
        out_shape=jax.ShapeDtypeStruct((T, IN_COLS), F32),
        scratch_shapes=[pltpu.VMEM((tm, D_MODEL), BF16)],
        compiler_params=_cparams(2),
        name="inproj",
    )(x2, norm_w.reshape(1, D_MODEL), w_bf)


def _retention_constants(seq):
    H, C = RET_HEADS, RET_CHUNK
    half = RET_DK // 2
    inv_freq = ROPE_BASE ** (-jnp.arange(half, dtype=F32) / half)
    ang = jnp.arange(seq, dtype=F32)[:, None] * inv_freq[None, :]
    cos, sin = jnp.cos(ang), jnp.sin(ang)
    cos_full = jnp.concatenate([cos, cos], axis=-1)
    sin_signed = jnp.concatenate([-sin, sin], axis=-1)
    log_g = jnp.log(1.0 - 2.0 ** (-5.0 - jnp.arange(H, dtype=F32)))
    i = jnp.arange(C, dtype=F32)
    diff = i[:, None] - i[None, :]
    dmask = jnp.where(diff[None] >= 0,
                      jnp.exp(log_g[:, None, None] * jnp.maximum(diff, 0.0)[None]), 0.0)
    xi = jnp.exp(log_g[:, None] * (i[None, :] + 1.0))
    zeta = jnp.exp(log_g[:, None] * (C - 1.0 - i)[None, :])
    g_chunk = jnp.exp(log_g * C)
    xi_b = jnp.broadcast_to(xi[:, :, None], (H, C, RET_DK))
    zeta_b = jnp.broadcast_to(zeta[:, :, None], (H, C, RET_DK))
    g_b = jnp.broadcast_to(g_chunk[:, None, None], (H, 8, RET_DV))
    return cos_full, sin_signed, dmask, xi_b, zeta_b, g_b


def _ret_kernel(q_ref, k_ref, v_ref, g_ref, cos_ref, sin_ref, dm_ref, xi_ref, ze_ref, gc_ref,
                o_ref, r_scr, *, n_chunks):
    @pl.when(pl.program_id(1) == 0)
    def _():
        r_scr[...] = jnp.zeros_like(r_scr)

    scale = RET_DK ** -0.5
    for c in range(n_chunks):
        rows = slice(c * RET_CHUNK, (c + 1) * RET_CHUNK)
        cos = cos_ref[rows, :]
        sin = sin_ref[rows, :]
        for h in range(RET_HEADS):
            kc = slice(h * RET_DK, (h + 1) * RET_DK)
            vc = slice(h * RET_DV, (h + 1) * RET_DV)
            q = q_ref[0, rows, kc]
            k = k_ref[0, rows, kc]
            qr = q * cos + pltpu.roll(q, RET_DK // 2, 1) * sin
            kr = (k * cos + pltpu.roll(k, RET_DK // 2, 1) * sin) * scale
            vb = v_ref[0, rows, vc].astype(BF16)
            sc = lax.dot_general(qr.astype(BF16), kr.astype(BF16), (((1,), (1,)), ((), ())),
                                 preferred_element_type=F32) * dm_ref[h]
            inner = jnp.dot(sc.astype(BF16), vb, preferred_element_type=F32)
            state = r_scr[h]
            cross = jnp.dot((qr * xi_ref[h]).astype(BF16), state.astype(BF16),
                            preferred_element_type=F32)
            kzt = (kr * ze_ref[h]).T.astype(BF16)
            r_scr[h] = gc_ref[h, 0:1, :] * state + jnp.dot(kzt, vb, preferred_element_type=F32)
            o = inner + cross
            mu = jnp.mean(o, axis=-1, keepdims=True)
            var = jnp.mean((o - mu) ** 2, axis=-1, keepdims=True)
            y = (o - mu) * lax.rsqrt(var + EPS)
            g = g_ref[0, rows, vc]
            o_ref[0, rows, vc] = ((g * jax.nn.sigmoid(g)) * y).astype(BF16)


def _retention(proj3, consts):
    B, S, _ = proj3.shape
    n_chunks = 2
    tm = n_chunks * RET_CHUNK
    cos_full, sin_signed, dmask, xi_b, zeta_b, g_b = consts
    H, C = RET_HEADS, RET_CHUNK
    full3 = lambda b, j: (0, 0, 0)
    return pl.pallas_call(
        functools.partial(_ret_kernel, n_chunks=n_chunks),
        grid=(B, S // tm),
        in_specs=[
            pl.BlockSpec((1, tm, Q_DIM), lambda b, j: (b, j, OFF_Q // Q_DIM)),
            pl.BlockSpec((1, tm, Q_DIM), lambda b, j: (b, j, OFF_K // Q_DIM)),
            pl.BlockSpec((1, tm, V_DIM), lambda b, j: (b, j, OFF_V // V_DIM)),
            pl.BlockSpec((1, tm, V_DIM), lambda b, j: (b, j, OFF_G // V_DIM)),
            pl.BlockSpec((tm, RET_DK), lambda b, j: (j, 0)),
            pl.BlockSpec((tm, RET_DK), lambda b, j: (j, 0)),
            pl.BlockSpec((H, C, C), full3),
            pl.BlockSpec((H, C, RET_DK), full3),
            pl.BlockSpec((H, C, RET_DK), full3),
            pl.BlockSpec((H, 8, RET_DV), full3),
        ],
        out_specs=pl.BlockSpec((1, tm, V_DIM), lambda b, j: (b, j, 0)),
        out_shape=jax.ShapeDtypeStruct((B, S, V_DIM), BF16),
        scratch_shapes=[pltpu.VMEM((H, RET_DK, RET_DV), F32)],
        compiler_params=_cparams(2),
        name="retention",
    )(proj3, proj3, proj3, proj3, cos_full, sin_signed, dmask, xi_b, zeta_b, g_b)


def _pool_kernel(cur_ref, halo_ref, wl_ref, sc_ref, o_ref, *, tm):
    j = pl.program_id(1)
    cur = cur_ref[0]
    halo = jnp.where(j == 0, 0.0, halo_ref[0])
    ext = jnp.concatenate([halo, cur], axis=0)
    pos = j * tm + lax.broadcasted_iota(I32, (tm, 1), 0)
    for g, w in enumerate(POOL_WINDOWS):
        cols = slice(g * POOL_GDIM, (g + 1) * POOL_GDIM)
        s = ext[:, cols]
        sh = 1
        while sh < w:
            s = s + pltpu.roll(s, sh, 0)
            sh *= 2
        win = s[POOL_HALO:, :]
        count = jnp.minimum(pos + 1, w).astype(F32)
        pooled = win / count - cur[:, cols]
        lin = jnp.dot(pooled.astype(BF16), wl_ref[g], preferred_element_type=F32)
        o_ref[0, :, cols] = (lin * sc_ref[:, cols]).astype(BF16)


def _pool(proj3, wl_bf, pool_scale):
    B, S, _ = proj3.shape
    tm = 512
    return pl.pallas_call(
        functools.partial(_pool_kernel, tm=tm),
        grid=(B, S // tm),
        in_specs=[
            pl.BlockSpec((1, tm, POOL_DIM), lambda b, j: (b, j, OFF_P // POOL_DIM)),
            pl.BlockSpec((1, POOL_HALO, POOL_DIM),
                         lambda b, j: (b, jnp.maximum(j * (tm // POOL_HALO) - 1, 0), OFF_P // POOL_DIM)),
            pl.BlockSpec((POOL_GROUPS, POOL_GDIM, POOL_GDIM), lambda b, j: (0, 0, 0)),
            pl.BlockSpec((1, POOL_DIM), lambda b, j: (0, 0)),
        ],
        out_specs=pl.BlockSpec((1, tm, POOL_DIM), lambda b, j: (b, j, 0)),
        out_shape=jax.ShapeDtypeStruct((B, S, POOL_DIM), BF16),
        compiler_params=_cparams(2),
        name="pool",
    )(proj3, proj3, wl_bf, pool_scale.reshape(1, POOL_DIM))


def _mix_kernel(ret_ref, wr_ref, pool_ref, wp_ref, gr_ref, gp_ref, o_ref):
    y_ret = jnp.dot(ret_ref[...], wr_ref[...], preferred_element_type=F32)
    y_pool = jnp.dot(pool_ref[...], wp_ref[...], preferred_element_type=F32)
    mixed = jax.nn.sigmoid(gr_ref[...]) * y_ret + jax.nn.sigmoid(gp_ref[...]) * y_pool
    o_ref[...] = mixed.astype(BF16)


def _mix(ret2, wr_bf, pool2, wp_bf, proj2):
    T = ret2.shape[0]
    tm, tn = 512, 512
    off_r = OFF_BG // tn
    off_p = (OFF_BG + D_MODEL) // tn
    return pl.pallas_call(
        _mix_kernel,
        grid=(T // tm, D_MODEL // tn),
        in_specs=[
            pl.BlockSpec((tm, V_DIM), lambda i, j: (i, 0)),
            pl.BlockSpec((V_DIM, tn), lambda i, j: (0, j)),
            pl.BlockSpec((tm, POOL_DIM), lambda i, j: (i, 0)),
            pl.BlockSpec((POOL_DIM, tn), lambda i, j: (0, j)),
            pl.BlockSpec((tm, tn), lambda i, j: (i, off_r + j)),
            pl.BlockSpec((tm, tn), lambda i, j: (i, off_p + j)),
        ],
        out_specs=pl.BlockSpec((tm, tn), lambda i, j: (i, j)),
        out_shape=jax.ShapeDtypeStruct((T, D_MODEL), BF16),
        compiler_params=_cparams(2),
        name="mix",
    )(ret2, wr_bf, pool2, wp_bf, proj2, proj2)


def _resid_kernel(m_ref, w_ref, x_ref, o_ref):
    o_ref[...] = x_ref[...] + jnp.dot(m_ref[...], w_ref[...], preferred_element_type=F32)


def _resid(mixed, w_bf, x2):
    T = x2.shape[0]
    tm, tn = 512, 512
    return pl.pallas_call(
        _resid_kernel,
        grid=(T // tm, D_MODEL // tn),
        in_specs=[
            pl.BlockSpec((tm, D_MODEL), lambda i, j: (i, 0)),
            pl.BlockSpec((D_MODEL, tn), lambda i, j: (0, j)),
            pl.BlockSpec((tm, tn), lambda i, j: (i, j)),
        ],
        out_specs=pl.BlockSpec((tm, tn), lambda i, j: (i, j)),
        out_shape=jax.ShapeDtypeStruct((T, D_MODEL), F32),
        compiler_params=_cparams(2),
        name="resid",
    )(mixed, w_bf, x2)


def _topk_axis0(s, k, iota, n):
    vals, idxs = [], []
    for _ in range(k):
        m = jnp.max(s, axis=0, keepdims=True)
        idx = jnp.min(jnp.where(s == m, iota, n), axis=0, keepdims=True)
        vals.append(m)
        idxs.append(idx)
        s = jnp.where(iota == idx, -jnp.inf, s)
    return jnp.concatenate(vals, axis=0), jnp.concatenate(idxs, axis=0)


def _peerq_kernel(x_ref, nw_ref, wq_ref, keys_ref, h_ref, e_ref, g_ref, q_scr, *, tm):
    x = x_ref[...]
    ms = jnp.mean(x * x, axis=-1, keepdims=True)
    hb = ((x * lax.rsqrt(ms + EPS)) * nw_ref[...]).astype(BF16)
    h_ref[...] = hb.astype(F32)
    q_scr[...] = jnp.dot(hb, wq_ref[...], preferred_element_type=F32).astype(BF16)

    K = PEER_TOPK
    iota_k = lax.broadcasted_iota(I32, (PEER_NKEYS, tm), 0)
    iota_c = lax.broadcasted_iota(I32, (K * K, tm), 0)

    def head(h, carry):
        sv, si = [], []
        for p in range(2):
            off = pl.multiple_of(h * PEER_DKEY + p * PEER_DHALF, PEER_DHALF)
            qhp = q_scr[:, pl.ds(off, PEER_DHALF)]
            s = lax.dot_general(keys_ref[p], qhp, (((1,), (1,)), ((), ())),
                                preferred_element_type=F32)
            v, i = _topk_axis0(s, K, iota_k, PEER_NKEYS)
            sv.append(v)
            si.append(i)
        cand = (sv[0][:, None, :] + sv[1][None, :, :]).reshape(K * K, tm)
        cidx = (si[0][:, None, :] * PEER_NKEYS + si[1][None, :, :]).reshape(K * K, tm)
        best, eidx = [], []
        for _ in range(K):
            m = jnp.max(cand, axis=0, keepdims=True)
            pos = jnp.min(jnp.where(cand == m, iota_c, K * K), axis=0, keepdims=True)
            hit = iota_c == pos
            best.append(m)
            eidx.append(jnp.max(jnp.where(hit, cidx, -1), axis=0, keepdims=True))
            cand = jnp.where(hit, -jnp.inf, cand)
        best = jnp.concatenate(best, axis=0)
        ex = jnp.exp(best - best[0:1, :])
        gate = ex / jnp.sum(ex, axis=0, keepdims=True)
        row = pl.multiple_of(h * K, K)
        e_ref[pl.ds(row, K), :] = jnp.concatenate(eidx, axis=0)
        g_ref[pl.ds(row, K), :] = gate
        return carry

    lax.fori_loop(0, PEER_HEADS, head, 0)


def _peerq(x1, norm_w, wq_bf, keys_bf):
    T = x1.shape[0]
    tm = 256
    return pl.pallas_call(
        functools.partial(_peerq_kernel, tm=tm),
        grid=(T // tm,),
        in_specs=[
            pl.BlockSpec((tm, D_MODEL), lambda i: (i, 0)),
            pl.BlockSpec((1, D_MODEL), lambda i: (0, 0)),
            pl.BlockSpec((D_MODEL, PEER_HEADS * PEER_DKEY), lambda i: (0, 0)),
            pl.BlockSpec((2, PEER_NKEYS, PEER_DHALF), lambda i: (0, 0, 0)),
        ],
        out_specs=[
            pl.BlockSpec((tm, D_MODEL), lambda i: (i, 0)),
            pl.BlockSpec((PEER_SEL, tm), lambda i: (0, i)),
            pl.BlockSpec((PEER_SEL, tm), lambda i: (0, i)),
        ],
        out_shape=[
            jax.ShapeDtypeStruct((T, D_MODEL), F32),
            jax.ShapeDtypeStruct((PEER_SEL, T), I32),
            jax.ShapeDtypeStruct((PEER_SEL, T), F32),
        ],
        scratch_shapes=[pltpu.VMEM((tm, PEER_HEADS * PEER_DKEY), BF16)],
        compiler_params=_cparams(1),
        name="peerq",
    )(x1, norm_w.reshape(1, D_MODEL), wq_bf, keys_bf)


PEER_SLOTS = 8
PEER_AHEAD = 6
PEER_TB = 128
PEER_NEXT = 8
PEER_NACC = 4


def _peer_kernel(idx_ref, gate_ref, h_ref, hn_ref, tab_ref, y_ref, *scratch):
    bufs = scratch[:PEER_SLOTS]
    a_scr, sem = scratch[PEER_SLOTS:]

    def issue(tok, slot):
        for r in range(PEER_SEL):
            pltpu.make_async_copy(tab_ref.at[idx_ref[0, tok, r]], bufs[slot].at[r], sem.at[slot]).start()

    def wait(slot):
        pltpu.make_async_copy(tab_ref.at[pl.ds(0, PEER_SEL)], bufs[slot], sem.at[slot]).wait()

    lane = lax.broadcasted_iota(I32, (8, LANES), 1)

    sub = lax.broadcasted_iota(I32, (8, LANES), 0)

    def weights(tok, a_part):
        a_row = jnp.sum(a_part, axis=0, keepdims=True)
        act = 0.5 * a_row * (1.0 + lax.erf(a_row * np.float32(math.sqrt(0.5))))
        w_row = (act * gate_ref[pl.ds(tok, 1), :]).astype(BF16).astype(F32)
        return jnp.broadcast_to(w_row, (8, LANES))

    def rows(h, u_slot, v_tok, v_slot, a_prev):
        accs = [jnp.zeros((8, LANES), F32) for _ in range(PEER_NACC)]
        if v_tok is not None:
            w8 = weights(v_tok, a_prev)
            ys = [jnp.zeros((ROW_SUB, LANES), F32) for _ in range(PEER_NACC)]
        for r in range(PEER_SEL):
            c = r % PEER_NACC
            p = bufs[u_slot][r, 0:ROW_SUB, :].astype(F32) * h
            s = jnp.sum(p[0:8] + p[8:16], axis=1, keepdims=True)
            accs[c] = jnp.where(lane == r, s, accs[c])
            if v_tok is not None:
                if r % 8 == 0:
                    w_col = jnp.sum(jnp.where(lane == sub + r, w8, 0.0), axis=1, keepdims=True)
                ys[c] = ys[c] + w_col[r % 8:r % 8 + 1, :] * bufs[v_slot][r, ROW_SUB:2 * ROW_SUB, :].astype(F32)
        if v_tok is not None:
            y_ref[v_tok] = (ys[0] + ys[1]) + (ys[2] + ys[3])
        return (accs[0] + accs[1]) + (accs[2] + accs[3])

    step_i = pl.program_id(0)
    n_groups = PEER_TB // PEER_SLOTS

    @pl.when(step_i == 0)
    def _():
        for t in range(PEER_AHEAD + 1):
            issue(t, t)
        wait(0)
        a_scr[...] = rows(h_ref[0], 0, None, None, None)

    def group(g, a_prev):
        for k in range(PEER_SLOTS):
            j = g * PEER_SLOTS + k
            wait((k + 1) % PEER_SLOTS)
            issue(j + (PEER_AHEAD + 1), (k + PEER_AHEAD + 1) % PEER_SLOTS)
            if k + 1 < PEER_SLOTS:
                h = h_ref[j + 1]
            else:
                h = jnp.where(g == n_groups - 1, hn_ref[0], h_ref[jnp.minimum(j + 1, PEER_TB - 1)])
            a_prev = rows(h, (k + 1) % PEER_SLOTS, j, k, a_prev)
        return a_prev

    a_scr[...] = lax.fori_loop(0, n_groups, group, a_scr[...])

    @pl.when(step_i == pl.num_programs(0) - 1)
    def _():
        for t in range(PEER_TB + 1, PEER_TB + PEER_AHEAD + 1):
            wait(t % PEER_SLOTS)


def _peer(eidx, gate, h3, table):
    T = eidx.shape[0]
    tb = PEER_TB
    nb = T // tb
    idx3 = eidx.reshape(nb, tb, PEER_SEL)
    idx_ext = jnp.concatenate([idx3, jnp.roll(idx3[:, :PEER_NEXT], -1, axis=0)], axis=1)
    hn_blocks = T // PEER_NEXT
    return pl.pallas_call(
        _peer_kernel,
        grid=(nb,),
        in_specs=[
            pl.BlockSpec((1, tb + PEER_NEXT, PEER_SEL), lambda i: (i, 0, 0), memory_space=pltpu.SMEM),
            pl.BlockSpec((tb, PEER_SEL), lambda i: (i, 0)),
            pl.BlockSpec((tb, ROW_SUB, LANES), lambda i: (i, 0, 0)),
            pl.BlockSpec((PEER_NEXT, ROW_SUB, LANES),
                         lambda i: (jnp.minimum((i + 1) * (tb // PEER_NEXT), hn_blocks - 1), 0, 0)),
            pl.BlockSpec(memory_space=pl.ANY),
        ],
        out_specs=pl.BlockSpec((tb, ROW_SUB, LANES), lambda i: (i, 0, 0)),
        out_shape=jax.ShapeDtypeStruct((T, ROW_SUB, LANES), F32),
        scratch_shapes=(
            [pltpu.VMEM((PEER_SEL, 2 * ROW_SUB, LANES), BF16) for _ in range(PEER_SLOTS)]
            + [pltpu.VMEM((8, LANES), F32), pltpu.SemaphoreType.DMA((PEER_SLOTS,))]),
        compiler_params=_cparams(1),
        name="peer",
    )(idx_ext, gate, h3, h3, table)


def _pack_table(u, v):
    E = u.shape[0]
    return jnp.concatenate([u.astype(BF16).reshape(E, ROW_SUB, LANES),
                            v.astype(BF16).reshape(E, ROW_SUB, LANES)], axis=1)


def _final_kernel(x_ref, y_ref, nw_ref, o_ref):
    x = x_ref[...] + y_ref[...]
    ms = jnp.mean(x * x, axis=-1, keepdims=True)
    o_ref[...] = (x * lax.rsqrt(ms + EPS)) * nw_ref[...]


def _final(x1, y, norm_w):
    T = x1.shape[0]
    tm = 512
    return pl.pallas_call(
        _final_kernel,
        grid=(T // tm,),
        in_specs=[
            pl.BlockSpec((tm, D_MODEL), lambda i: (i, 0)),
            pl.BlockSpec((tm, D_MODEL), lambda i: (i, 0)),
            pl.BlockSpec((1, D_MODEL), lambda i: (0, 0)),
        ],
        out_specs=pl.BlockSpec((tm, D_MODEL), lambda i: (i, 0)),
        out_shape=jax.ShapeDtypeStruct((T, D_MODEL), F32),
        compiler_params=_cparams(1),
        name="final",
    )(x1, y, norm_w.reshape(1, D_MODEL))


def kernel(x, norm1_w, w_in, w_ret_o, w_pool_lin, pool_scale, w_pool_o, w_out,
           norm2_w, peer_w_q, peer_sub_keys, peer_u, peer_v, final_norm_w):
    B, S, D = x.shape
    T = B * S
    depth = w_in.shape[0]
    consts = _retention_constants(S)
    x2 = x.reshape(T, D)
    for l in range(depth):
        proj = _inproj(x2, norm1_w[l], w_in[l].astype(BF16))
        proj3 = proj.reshape(B, S, IN_COLS)
        ret = _retention(proj3, consts)
        pooled = _pool(proj3, w_pool_lin[l].astype(BF16), pool_scale[l])
        mixed = _mix(ret.reshape(T, V_DIM), w_ret_o[l].astype(BF16),
                     pooled.reshape(T, POOL_DIM), w_pool_o[l].astype(BF16), proj)
        x1 = _resid(mixed, w_out[l].astype(BF16), x2)
        h2, eidx_t, gate_t = _peerq(x1, norm2_w[l], peer_w_q[l].astype(BF16),
                                    peer_sub_keys[l].astype(BF16))
        table = _pack_table(peer_u[l], peer_v[l])
        y = _peer(eidx_t.T, gate_t.T, h2.reshape(T, ROW_SUB, LANES), table)
        x2 = x1 + y.reshape(T, D) if l + 1 < depth else x1
        y_last = y.reshape(T, D)
    out = _final(x2, y_last, final_norm_w)
    return out.reshape(B, S, D)
```

```python
import functools
import math

import numpy as np
import jax
import jax.numpy as jnp
from jax import lax
from jax.experimental import pallas as pl
from jax.experimental.pallas import tpu as pltpu

F32 = jnp.float32
BF16 = jnp.bfloat16
I32 = jnp.int32

D_MODEL = 2048
RET_HEADS = 8
RET_DK = 128
RET_DV = 256
RET_CHUNK = 128
POOL_WINDOWS = (2, 4, 8, 16)
POOL_GROUPS = len(POOL_WINDOWS)
POOL_DIM = 1024
POOL_GDIM = POOL_DIM // POOL_GROUPS
POOL_HALO = 16
PEER_HEADS = 8
PEER_NKEYS = 128
PEER_DKEY = 256
PEER_DHALF = PEER_DKEY // 2
PEER_TOPK = 16
PEER_SEL = PEER_HEADS * PEER_TOPK
ROPE_BASE = 10000.0
EPS = 1e-6

Q_DIM = RET_HEADS * RET_DK
V_DIM = RET_HEADS * RET_DV
OFF_Q = 0
OFF_K = OFF_Q + Q_DIM
OFF_V = OFF_K + Q_DIM
OFF_G = OFF_V + V_DIM
OFF_P = OFF_G + V_DIM
OFF_BG = OFF_P + POOL_DIM
IN_COLS = OFF_BG + 2 * D_MODEL

LANES = 128
ROW_SUB = D_MODEL // LANES
VMEM_LIMIT = 48 * 1024 * 1024


def _cparams(n_axes):
    return pltpu.CompilerParams(
        dimension_semantics=("arbitrary",) * n_axes, vmem_limit_bytes=VMEM_LIMIT)


def _inproj_kernel(x_ref, nw_ref, w_ref, o_ref, h_scr):
    @pl.when(pl.program_id(1) == 0)
    def _():
        x = x_ref[...]
        ms = jnp.mean(x * x, axis=-1, keepdims=True)
        h_scr[...] = ((x * lax.rsqrt(ms + EPS)) * nw_ref[...]).astype(BF16)

    o_ref[...] = jnp.dot(h_scr[...], w_ref[...], preferred_element_type=F32)


def _inproj(x2, norm_w, w_bf):
    T = x2.shape[0]
    tm, tn = 512, 1024
    return pl.pallas_call(
        _inproj_kernel,
        grid=(T // tm, IN_COLS // tn),
        in_specs=[
            pl.BlockSpec((tm, D_MODEL), lambda i, j: (i, 0)),
            pl.BlockSpec((1, D_MODEL), lambda i, j: (0, 0)),
            pl.BlockSpec((D_MODEL, tn), lambda i, j: (0, j)),
        ],
        out_specs=pl.BlockSpec((tm, tn), lambda i, j: (i, j)),
        out_shape=jax.ShapeDtypeStruct((T, IN_COLS), F32),
        scratch_shapes=[pltpu.VMEM((tm, D_MODEL), BF16)],
        compiler_params=_cparams(2),
        name="inproj",
    )(x2, norm_w.reshape(1, D_MODEL), w_bf)


def _retention_constants(seq):
    H, C = RET_HEADS, RET_CHUNK
    half = RET_DK // 2
    inv_freq = ROPE_BASE ** (-jnp.arange(half, dtype=F32) / half)
    ang = jnp.arange(seq, dtype=F32)[:, None] * inv_freq[None, :]
    cos, sin = jnp.cos(ang), jnp.sin(ang)
    cos_full = jnp.concatenate([cos, cos], axis=-1)
    sin_signed = jnp.concatenate([-sin, sin], axis=-1)
    log_g = jnp.log(1.0 - 2.0 ** (-5.0 - jnp.arange(H, dtype=F32)))
    i = jnp.arange(C, dtype=F32)
    diff = i[:, None] - i[None, :]
    dmask = jnp.where(diff[None] >= 0,
                      jnp.exp(log_g[:, None, None] * jnp.maximum(diff, 0.0)[None]), 0.0)
    xi = jnp.exp(log_g[:, None] * (i[None, :] + 1.0))
    zeta = jnp.exp(log_g[:, None] * (C - 1.0 - i)[None, :])
    g_chunk = jnp.exp(log_g * C)
    xi_b = jnp.broadcast_to(xi[:, :, None], (H, C, RET_DK))
    zeta_b = jnp.broadcast_to(zeta[:, :, None], (H, C, RET_DK))
    g_b = jnp.broadcast_to(g_chunk[:, None, None], (H, 8, RET_DV))
    return cos_full, sin_signed, dmask, xi_b, zeta_b, g_b


def _ret_kernel(q_ref, k_ref, v_ref, g_ref, cos_ref, sin_ref, dm_ref, xi_ref, ze_ref, gc_ref,
                o_ref, r_scr, *, n_chunks):
    @pl.when(pl.program_id(1) == 0)
    def _():
        r_scr[...] = jnp.zeros_like(r_scr)

    scale = RET_DK ** -0.5
    for c in range(n_chunks):
        rows = slice(c * RET_CHUNK, (c + 1) * RET_CHUNK)
        cos = cos_ref[rows, :]
        sin = sin_ref[rows, :]
        for h in range(RET_HEADS):
            kc = slice(h * RET_DK, (h + 1) * RET_DK)
            vc = slice(h * RET_DV, (h + 1) * RET_DV)
            q = q_ref[0, rows, kc]
            k = k_ref[0, rows, kc]
            qr = q * cos + pltpu.roll(q, RET_DK // 2, 1) * sin
            kr = (k * cos + pltpu.roll(k, RET_DK // 2, 1) * sin) * scale
            vb = v_ref[0, rows, vc].astype(BF16)
            sc = lax.dot_general(qr.astype(BF16), kr.astype(BF16), (((1,), (1,)), ((), ())),
                                 preferred_element_type=F32) * dm_ref[h]
            inner = jnp.dot(sc.astype(BF16), vb, preferred_element_type=F32)
            state = r_scr[h]
            cross = jnp.dot((qr * xi_ref[h]).astype(BF16), state.astype(BF16),
                            preferred_element_type=F32)
            kzt = (kr * ze_ref[h]).T.astype(BF16)
            r_scr[h] = gc_ref[h, 0:1, :] * state + jnp.dot(kzt, vb, preferred_element_type=F32)
            o = inner + cross
            mu = jnp.mean(o, axis=-1, keepdims=True)
            var = jnp.mean((o - mu) ** 2, axis=-1, keepdims=True)
            y = (o - mu) * lax.rsqrt(var + EPS)
            g = g_ref[0, rows, vc]
            o_ref[0, rows, vc] = ((g * jax.nn.sigmoid(g)) * y).astype(BF16)


def _retention(proj3, consts):
    B, S, _ = proj3.shape
    n_chunks = 2
    tm = n_chunks * RET_CHUNK
    cos_full, sin_signed, dmask, xi_b, zeta_b, g_b = consts
    H, C = RET_HEADS, RET_CHUNK
    full3 = lambda b, j: (0, 0, 0)
    return pl.pallas_call(
        functools.partial(_ret_kernel, n_chunks=n_chunks),
        grid=(B, S // tm),
        in_specs=[
            pl.BlockSpec((1, tm, Q_DIM), lambda b, j: (b, j, OFF_Q // Q_DIM)),
            pl.BlockSpec((1, tm, Q_DIM), lambda b, j: (b, j, OFF_K // Q_DIM)),
            pl.BlockSpec((1, tm, V_DIM), lambda b, j: (b, j, OFF_V // V_DIM)),
            pl.BlockSpec((1, tm, V_DIM), lambda b, j: (b, j, OFF_G // V_DIM)),
            pl.BlockSpec((tm, RET_DK), lambda b, j: (j, 0)),
            pl.BlockSpec((tm, RET_DK), lambda b, j: (j, 0)),
            pl.BlockSpec((H, C, C), full3),
            pl.BlockSpec((H, C, RET_DK), full3),
            pl.BlockSpec((H, C, RET_DK), full3),
            pl.BlockSpec((H, 8, RET_DV), full3),
        ],
        out_specs=pl.BlockSpec((1, tm, V_DIM), lambda b, j: (b, j, 0)),
        out_shape=jax.ShapeDtypeStruct((B, S, V_DIM), BF16),
        scratch_shapes=[pltpu.VMEM((H, RET_DK, RET_DV), F32)],
        compiler_params=_cparams(2),
        name="retention",
    )(proj3, proj3, proj3, proj3, cos_full, sin_signed, dmask, xi_b, zeta_b, g_b)


def _pool_kernel(cur_ref, halo_ref, wl_ref, sc_ref, o_ref, *, tm):
    j = pl.program_id(1)
    cur = cur_ref[0]
    halo = jnp.where(j == 0, 0.0, halo_ref[0])
    ext = jnp.concatenate([halo, cur], axis=0)
    pos = j * tm + lax.broadcasted_iota(I32, (tm, 1), 0)
    for g, w in enumerate(POOL_WINDOWS):
        cols = slice(g * POOL_GDIM, (g + 1) * POOL_GDIM)
        s = ext[:, cols]
        sh = 1
        while sh < w:
            s = s + pltpu.roll(s, sh, 0)
            sh *= 2
        win = s[POOL_HALO:, :]
        count = jnp.minimum(pos + 1, w).astype(F32)
        pooled = win / count - cur[:, cols]
        lin = jnp.dot(pooled.astype(BF16), wl_ref[g], preferred_element_type=F32)
        o_ref[0, :, cols] = (lin * sc_ref[:, cols]).astype(BF16)


def _pool(proj3, wl_bf, pool_scale):
    B, S, _ = proj3.shape
    tm = 512
    return pl.pallas_call(
        functools.partial(_pool_kernel, tm=tm),
        grid=(B, S // tm),
        in_specs=[
            pl.BlockSpec((1, tm, POOL_DIM), lambda b, j: (b, j, OFF_P // POOL_DIM)),
            pl.BlockSpec((1, POOL_HALO, POOL_DIM),
                         lambda b, j: (b, jnp.maximum(j * (tm // POOL_HALO) - 1, 0), OFF_P // POOL_DIM)),
            pl.BlockSpec((POOL_GROUPS, POOL_GDIM, POOL_GDIM), lambda b, j: (0, 0, 0)),
            pl.BlockSpec((1, POOL_DIM), lambda b, j: (0, 0)),
        ],
        out_specs=pl.BlockSpec((1, tm, POOL_DIM), lambda b, j: (b, j, 0)),
        out_shape=jax.ShapeDtypeStruct((B, S, POOL_DIM), BF16),
        compiler_params=_cparams(2),
        name="pool",
    )(proj3, proj3, wl_bf, pool_scale.reshape(1, POOL_DIM))


def _mix_kernel(ret_ref, wr_ref, pool_ref, wp_ref, gr_ref, gp_ref, o_ref):
    y_ret = jnp.dot(ret_ref[...], wr_ref[...], preferred_element_type=F32)
    y_pool = jnp.dot(pool_ref[...], wp_ref[...], preferred_element_type=F32)
    mixed = jax.nn.sigmoid(gr_ref[...]) * y_ret + jax.nn.sigmoid(gp_ref[...]) * y_pool
    o_ref[...] = mixed.astype(BF16)


def _mix(ret2, wr_bf, pool2, wp_bf, proj2):
    T = ret2.shape[0]
    tm, tn = 512, 512
    off_r = OFF_BG // tn
    off_p = (OFF_BG + D_MODEL) // tn
    return pl.pallas_call(
        _mix_kernel,
        grid=(T // tm, D_MODEL // tn),
        in_specs=[
            pl.BlockSpec((tm, V_DIM), lambda i, j: (i, 0)),
            pl.BlockSpec((V_DIM, tn), lambda i, j: (0, j)),
            pl.BlockSpec((tm, POOL_DIM), lambda i, j: (i, 0)),
            pl.BlockSpec((POOL_DIM, tn), lambda i, j: (0, j)),
            pl.BlockSpec((tm, tn), lambda i, j: (i, off_r + j)),
            pl.BlockSpec((tm, tn), lambda i, j: (i, off_p + j)),
        ],
        out_specs=pl.BlockSpec((tm, tn), lambda i, j: (i, j)),
        out_shape=jax.ShapeDtypeStruct((T, D_MODEL), BF16),
        compiler_params=_cparams(2),
        name="mix",
    )(ret2, wr_bf, pool2, wp_bf, proj2, proj2)


def _resid_kernel(m_ref, w_ref, x_ref, o_ref):
    o_ref[...] = x_ref[...] + jnp.dot(m_ref[...], w_ref[...], preferred_element_type=F32)


def _resid(mixed, w_bf, x2):
    T = x2.shape[0]
    tm, tn = 512, 512
    return pl.pallas_call(
        _resid_kernel,
        grid=(T // tm, D_MODEL // tn),
        in_specs=[
            pl.BlockSpec((tm, D_MODEL), lambda i, j: (i, 0)),
            pl.BlockSpec((D_MODEL, tn), lambda i, j: (0, j)),
            pl.BlockSpec((tm, tn), lambda i, j: (i, j)),
        ],
        out_specs=pl.BlockSpec((tm, tn), lambda i, j: (i, j)),
        out_shape=jax.ShapeDtypeStruct((T, D_MODEL), F32),
        compiler_params=_cparams(2),
        name="resid",
    )(mixed, w_bf, x2)


def _topk_axis0(s, k, iota, n):
    vals, idxs = [], []
    for _ in range(k):
        m = jnp.max(s, axis=0, keepdims=True)
        idx = jnp.min(jnp.where(s == m, iota, n), axis=0, keepdims=True)
        vals.append(m)
        idxs.append(idx)
        s = jnp.where(iota == idx, -jnp.inf, s)
    return jnp.concatenate(vals, axis=0), jnp.concatenate(idxs, axis=0)


def _peerq_kernel(x_ref, nw_ref, wq_ref, keys_ref, h_ref, e_ref, g_ref, q_scr, *, tm):
    x = x_ref[...]
    ms = jnp.mean(x * x, axis=-1, keepdims=True)
    hb = ((x * lax.rsqrt(ms + EPS)) * nw_ref[...]).astype(BF16)
    h_ref[...] = hb.astype(F32)
    q_scr[...] = jnp.dot(hb, wq_ref[...], preferred_element_type=F32).astype(BF16)

    K = PEER_TOPK
    iota_k = lax.broadcasted_iota(I32, (PEER_NKEYS, tm), 0)
    iota_c = lax.broadcasted_iota(I32, (K * K, tm), 0)

    def head(h, carry):
        sv, si = [], []
        for p in range(2):
            off = pl.multiple_of(h * PEER_DKEY + p * PEER_DHALF, PEER_DHALF)
            qhp = q_scr[:, pl.ds(off, PEER_DHALF)]
            s = lax.dot_general(keys_ref[p], qhp, (((1,), (1,)), ((), ())),
                                preferred_element_type=F32)
            v, i = _topk_axis0(s, K, iota_k, PEER_NKEYS)
            sv.append(v)
            si.append(i)
        cand = (sv[0][:, None, :] + sv[1][None, :, :]).reshape(K * K, tm)
        cidx = (si[0][:, None, :] * PEER_NKEYS + si[1][None, :, :]).reshape(K * K, tm)
        best, eidx = [], []
        for _ in range(K):
            m = jnp.max(cand, axis=0, keepdims=True)
            pos = jnp.min(jnp.where(cand == m, iota_c, K * K), axis=0, keepdims=True)
            hit = iota_c == pos
            best.append(m)
            eidx.append(jnp.max(jnp.where(hit, cidx, -1), axis=0, keepdims=True))
            cand = jnp.where(hit, -jnp.inf, cand)
        best = jnp.concatenate(best, axis=0)
        ex = jnp.exp(best - best[0:1, :])
        gate = ex / jnp.sum(ex, axis=0, keepdims=True)
        row = pl.multiple_of(h * K, K)
        e_ref[pl.ds(row, K), :] = jnp.concatenate(eidx, axis=0)
        g_ref[pl.ds(row, K), :] = gate
        return carry

    lax.fori_loop(0, PEER_HEADS, head, 0)


def _peerq(x1, norm_w, wq_bf, keys_bf):
    T = x1.shape[0]
    tm = 256
    return pl.pallas_call(
        functools.partial(_peerq_kernel, tm=tm),
        grid=(T // tm,),
        in_specs=[
            pl.BlockSpec((tm, D_MODEL), lambda i: (i, 0)),
            pl.BlockSpec((1, D_MODEL), lambda i: (0, 0)),
            pl.BlockSpec((D_MODEL, PEER_HEADS * PEER_DKEY), lambda i: (0, 0)),
            pl.BlockSpec((2, PEER_NKEYS, PEER_DHALF), lambda i: (0, 0, 0)),
        ],
        out_specs=[
            pl.BlockSpec((tm, D_MODEL), lambda i: (i, 0)),
            pl.BlockSpec((PEER_SEL, tm), lambda i: (0, i)),
            pl.BlockSpec((PEER_SEL, tm), lambda i: (0, i)),
        ],
        out_shape=[
            jax.ShapeDtypeStruct((T, D_MODEL), F32),
            jax.ShapeDtypeStruct((PEER_SEL, T), I32),
            jax.ShapeDtypeStruct((PEER_SEL, T), F32),
        ],
        scratch_shapes=[pltpu.VMEM((tm, PEER_HEADS * PEER_DKEY), BF16)],
        compiler_params=_cparams(1),
        name="peerq",
    )(x1, norm_w.reshape(1, D_MODEL), wq_bf, keys_bf)


PEER_SLOTS = 8
PEER_AHEAD = 6
PEER_TB = 128
PEER_NEXT = 8
PEER_NACC = 4


def _peer_kernel(idx_ref, gate_ref, h_ref, hn_ref, tab_ref, y_ref, *scratch):
    bufs = scratch[:PEER_SLOTS]
    a_scr, sem = scratch[PEER_SLOTS:]

    def issue(tok, slot):
        for r in range(PEER_SEL):
            pltpu.make_async_copy(tab_ref.at[idx_ref[0, tok, r]], bufs[slot].at[r],
                                  sem.at[slot]).start(priority=r % 2)

    def wait(slot):
        pltpu.make_async_copy(tab_ref.at[pl.ds(0, PEER_SEL)], bufs[slot], sem.at[slot]).wait()

    lane = lax.broadcasted_iota(I32, (8, LANES), 1)

    sub = lax.broadcasted_iota(I32, (8, LANES), 0)

    def weights(tok, a_part):
        a_row = jnp.sum(a_part, axis=0, keepdims=True)
        act = 0.5 * a_row * (1.0 + lax.erf(a_row * np.float32(math.sqrt(0.5))))
        w_row = (act * gate_ref[pl.ds(tok, 1), :]).astype(BF16).astype(F32)
        return jnp.broadcast_to(w_row, (8, LANES))

    def rows(h, u_slot, v_tok, v_slot, a_prev):
        accs = [jnp.zeros((8, LANES), F32) for _ in range(PEER_NACC)]
        if v_tok is not None:
            w8 = weights(v_tok, a_prev)
            ys = [jnp.zeros((ROW_SUB, LANES), F32) for _ in range(PEER_NACC)]
        for r in range(PEER_SEL):
            c = r % PEER_NACC
            p = bufs[u_slot][r, 0:ROW_SUB, :].astype(F32) * h
            s = jnp.sum(p[0:8] + p[8:16], axis=1, keepdims=True)
            accs[c] = jnp.where(lane == r, s, accs[c])
            if v_tok is not None:
                if r % 8 == 0:
                    w_col = jnp.sum(jnp.where(lane == sub + r, w8, 0.0), axis=1, keepdims=True)
                ys[c] = ys[c] + w_col[r % 8:r % 8 + 1, :] * bufs[v_slot][r, ROW_SUB:2 * ROW_SUB, :].astype(F32)
        if v_tok is not None:
            y_ref[v_tok] = (ys[0] + ys[1]) + (ys[2] + ys[3])
        return (accs[0] + accs[1]) + (accs[2] + accs[3])

    step_i = pl.program_id(0)
    n_groups = PEER_TB // PEER_SLOTS

    @pl.when(step_i == 0)
    def _():
        for t in range(PEER_AHEAD + 1):
            issue(t, t)
        wait(0)
        a_scr[...] = rows(h_ref[0], 0, None, None, None)

    def group(g, a_prev):
        for k in range(PEER_SLOTS):
            j = g * PEER_SLOTS + k
            wait((k + 1) % PEER_SLOTS)
            issue(j + (PEER_AHEAD + 1), (k + PEER_AHEAD + 1) % PEER_SLOTS)
            if k + 1 < PEER_SLOTS:
                h = h_ref[j + 1]
            else:
                h = jnp.where(g == n_groups - 1, hn_ref[0], h_ref[jnp.minimum(j + 1, PEER_TB - 1)])
            a_prev = rows(h, (k + 1) % PEER_SLOTS, j, k, a_prev)
        return a_prev

    a_scr[...] = lax.fori_loop(0, n_groups, group, a_scr[...])

    @pl.when(step_i == pl.num_programs(0) - 1)
    def _():
        for t in range(PEER_TB + 1, PEER_TB + PEER_AHEAD + 1):
            wait(t % PEER_SLOTS)


def _peer(eidx, gate, h3, table):
    T = eidx.shape[0]
    tb = PEER_TB
    nb = T // tb
    idx3 = eidx.reshape(nb, tb, PEER_SEL)
    idx_ext = jnp.concatenate([idx3, jnp.roll(idx3[:, :PEER_NEXT], -1, axis=0)], axis=1)
    hn_blocks = T // PEER_NEXT
    return pl.pallas_call(
        _peer_kernel,
        grid=(nb,),
        in_specs=[
            pl.BlockSpec((1, tb + PEER_NEXT, PEER_SEL), lambda i: (i, 0, 0), memory_space=pltpu.SMEM),
            pl.BlockSpec((tb, PEER_SEL), lambda i: (i, 0)),
            pl.BlockSpec((tb, ROW_SUB, LANES), lambda i: (i, 0, 0)),
            pl.BlockSpec((PEER_NEXT, ROW_SUB, LANES),
                         lambda i: (jnp.minimum((i + 1) * (tb // PEER_NEXT), hn_blocks - 1), 0, 0)),
            pl.BlockSpec(memory_space=pl.ANY),
        ],
        out_specs=pl.BlockSpec((tb, ROW_SUB, LANES), lambda i: (i, 0, 0)),
        out_shape=jax.ShapeDtypeStruct((T, ROW_SUB, LANES), F32),
        scratch_shapes=(
            [pltpu.VMEM((PEER_SEL, 2 * ROW_SUB, LANES), BF16) for _ in range(PEER_SLOTS)]
            + [pltpu.VMEM((8, LANES), F32), pltpu.SemaphoreType.DMA((PEER_SLOTS,))]),
        compiler_params=_cparams(1),
        name="peer",
    )(idx_ext, gate, h3, h3, table)


def _pack_table(u, v):
    E = u.shape[0]
    return jnp.concatenate([u.astype(BF16).reshape(E, ROW_SUB, LANES),
                            v.astype(BF16).reshape(E, ROW_SUB, LANES)], axis=1)


def _final_kernel(x_ref, y_ref, nw_ref, o_ref):
    x = x_ref[...] + y_ref[...]
    ms = jnp.mean(x * x, axis=-1, keepdims=True)
    o_ref[...] = (x * lax.rsqrt(ms + EPS)) * nw_ref[...]


def _final(x1, y, norm_w):
    T = x1.shape[0]
    tm = 512
    return pl.pallas_call(
        _final_kernel,
        grid=(T // tm,),
        in_specs=[
            pl.BlockSpec((tm, D_MODEL), lambda i: (i, 0)),
            pl.BlockSpec((tm, D_MODEL), lambda i: (i, 0)),
            pl.BlockSpec((1, D_MODEL), lambda i: (0, 0)),
        ],
        out_specs=pl.BlockSpec((tm, D_MODEL), lambda i: (i, 0)),
        out_shape=jax.ShapeDtypeStruct((T, D_MODEL), F32),
        compiler_params=_cparams(1),
        name="final",
    )(x1, y, norm_w.reshape(1, D_MODEL))


def kernel(x, norm1_w, w_in, w_ret_o, w_pool_lin, pool_scale, w_pool_o, w_out,
           norm2_w, peer_w_q, peer_sub_keys, peer_u, peer_v, final_norm_w):
    B, S, D = x.shape
    T = B * S
    depth = w_in.shape[0]
    consts = _retention_constants(S)
    x2 = x.reshape(T, D)
    for l in range(depth):
        proj = _inproj(x2, norm1_w[l], w_in[l].astype(BF16))
        proj3 = proj.reshape(B, S, IN_COLS)
        ret = _retention(proj3, consts)
        pooled = _pool(proj3, w_pool_lin[l].astype(BF16), pool_scale[l])
        mixed = _mix(ret.reshape(T, V_DIM), w_ret_o[l].astype(BF16),
                     pooled.reshape(T, POOL_DIM), w_pool_o[l].astype(BF16), proj)
        x1 = _resid(mixed, w_out[l].astype(BF16), x2)
        h2, eidx_t, gate_t = _peerq(x1, norm2_w[l], peer_w_q[l].astype(BF16),
                                    peer_sub_keys[l].astype(BF16))
        table = _pack_table(peer_u[l], peer_v[l])
        y = _peer(eidx_t.T, gate_t.T, h2.reshape(T, ROW_SUB, LANES), table)
        x2 = x1 + y.reshape(T, D) if l + 1 < depth else x1
        y_last = y.reshape(T, D)
    out = _final(x2, y_last, final_norm_w)
    return out.reshape(B, S, D)
```

```python
import functools
import math

import numpy as np
import jax
import jax.numpy as jnp
from jax import lax
from jax.experimental import pallas as pl
from jax.experimental.pallas import tpu as pltpu

F32 = jnp.float32
BF16 = jnp.bfloat16
I32 = jnp.int32

D_MODEL = 2048
RET_HEADS = 8
RET_DK = 128
RET_DV = 256
RET_CHUNK = 128
POOL_WINDOWS = (2, 4, 8, 16)
POOL_GROUPS = len(POOL_WINDOWS)
POOL_DIM = 1024
POOL_GDIM = POOL_DIM // POOL_GROUPS
POOL_HALO = 16
PEER_HEADS = 8
PEER_NKEYS = 128
PEER_DKEY = 256
PEER_DHALF = PEER_DKEY // 2
PEER_TOPK = 16
PEER_SEL = PEER_HEADS * PEER_TOPK
ROPE_BASE = 10000.0
EPS = 1e-6

Q_DIM = RET_HEADS * RET_DK
V_DIM = RET_HEADS * RET_DV
OFF_Q = 0
OFF_K = OFF_Q + Q_DIM
OFF_V = OFF_K + Q_DIM
OFF_G = OFF_V + V_DIM
OFF_P = OFF_G + V_DIM
OFF_BG = OFF_P + POOL_DIM
IN_COLS = OFF_BG + 2 * D_MODEL

LANES = 128
ROW_SUB = D_MODEL // LANES
VMEM_LIMIT = 48 * 1024 * 1024


def _cparams(n_axes):
    return pltpu.CompilerParams(
        dimension_semantics=("arbitrary",) * n_axes, vmem_limit_bytes=VMEM_LIMIT)


def _inproj_kernel(x_ref, nw_ref, w_ref, o_ref, h_scr):
    @pl.when(pl.program_id(1) == 0)
    def _():
        x = x_ref[...]
        ms = jnp.mean(x * x, axis=-1, keepdims=True)
        h_scr[...] = ((x * lax.rsqrt(ms + EPS)) * nw_ref[...]).astype(BF16)

    o_ref[...] = jnp.dot(h_scr[...], w_ref[...], preferred_element_type=F32)


def _inproj(x2, norm_w, w_bf):
    T = x2.shape[0]
    tm, tn = 1024, 1024
    return pl.pallas_call(
        _inproj_kernel,
        grid=(T // tm, IN_COLS // tn),
        in_specs=[
            pl.BlockSpec((tm, D_MODEL), lambda i, j: (i, 0)),
            pl.BlockSpec((1, D_MODEL), lambda i, j: (0, 0)),
            pl.BlockSpec((D_MODEL, tn), lambda i, j: (0, j)),
        ],
        out_specs=pl.BlockSpec((tm, tn), lambda i, j: (i, j)),
        out_shape=jax.ShapeDtypeStruct((T, IN_COLS), F32),
        scratch_shapes=[pltpu.VMEM((tm, D_MODEL), BF16)],
        compiler_params=_cparams(2),
        name="inproj",
    )(x2, norm_w.reshape(1, D_MODEL), w_bf)


def _retention_constants(seq):
    H, C = RET_HEADS, RET_CHUNK
    half = RET_DK // 2
    inv_freq = ROPE_BASE ** (-jnp.arange(half, dtype=F32) / half)
    ang = jnp.arange(seq, dtype=F32)[:, None] * inv_freq[None, :]
    cos, sin = jnp.cos(ang), jnp.sin(ang)
    cos_full = jnp.concatenate([cos, cos], axis=-1)
    sin_signed = jnp.concatenate([-sin, sin], axis=-1)
    log_g = jnp.log(1.0 - 2.0 ** (-5.0 - jnp.arange(H, dtype=F32)))
    i = jnp.arange(C, dtype=F32)
    diff = i[:, None] - i[None, :]
    dmask = jnp.where(diff[None] >= 0,
                      jnp.exp(log_g[:, None, None] * jnp.maximum(diff, 0.0)[None]), 0.0)
    xi = jnp.exp(log_g[:, None] * (i[None, :] + 1.0))
    zeta = jnp.exp(log_g[:, None] * (C - 1.0 - i)[None, :])
    g_chunk = jnp.exp(log_g * C)
    xi_b = jnp.broadcast_to(xi[:, :, None], (H, C, RET_DK))
    zeta_b = jnp.broadcast_to(zeta[:, :, None], (H, C, RET_DK))
    g_b = jnp.broadcast_to(g_chunk[:, None, None], (H, 8, RET_DV))
    return cos_full, sin_signed, dmask, xi_b, zeta_b, g_b


def _ret_kernel(q_ref, k_ref, v_ref, g_ref, cos_ref, sin_ref, dm_ref, xi_ref, ze_ref, gc_ref,
                o_ref, r_scr, *, n_chunks):
    @pl.when(pl.program_id(1) == 0)
    def _():
        r_scr[...] = jnp.zeros_like(r_scr)

    scale = RET_DK ** -0.5
    for c in range(n_chunks):
        rows = slice(c * RET_CHUNK, (c + 1) * RET_CHUNK)
        cos = cos_ref[rows, :]
        sin = sin_ref[rows, :]
        for h in range(RET_HEADS):
            kc = slice(h * RET_DK, (h + 1) * RET_DK)
            vc = slice(h * RET_DV, (h + 1) * RET_DV)
            q = q_ref[0, rows, kc]
            k = k_ref[0, rows, kc]
            qr = q * cos + pltpu.roll(q, RET_DK // 2, 1) * sin
            kr = (k * cos + pltpu.roll(k, RET_DK // 2, 1) * sin) * scale
            vb = v_ref[0, rows, vc].astype(BF16)
            sc = lax.dot_general(qr.astype(BF16), kr.astype(BF16), (((1,), (1,)), ((), ())),
                                 preferred_element_type=F32) * dm_ref[h]
            inner = jnp.dot(sc.astype(BF16), vb, preferred_element_type=F32)
            state = r_scr[h]
            cross = jnp.dot((qr * xi_ref[h]).astype(BF16), state.astype(BF16),
                            preferred_element_type=F32)
            kzt = (kr * ze_ref[h]).T.astype(BF16)
            r_scr[h] = gc_ref[h, 0:1, :] * state + jnp.dot(kzt, vb, preferred_element_type=F32)
            o = inner + cross
            mu = jnp.mean(o, axis=-1, keepdims=True)
            var = jnp.mean((o - mu) ** 2, axis=-1, keepdims=True)
            y = (o - mu) * lax.rsqrt(var + EPS)
            g = g_ref[0, rows, vc]
            o_ref[0, rows, vc] = ((g * jax.nn.sigmoid(g)) * y).astype(BF16)


def _retention(proj3, consts):
    B, S, _ = proj3.shape
    n_chunks = 2
    tm = n_chunks * RET_CHUNK
    cos_full, sin_signed, dmask, xi_b, zeta_b, g_b = consts
    H, C = RET_HEADS, RET_CHUNK
    full3 = lambda b, j: (0, 0, 0)
    return pl.pallas_call(
        functools.partial(_ret_kernel, n_chunks=n_chunks),
        grid=(B, S // tm),
        in_specs=[
            pl.BlockSpec((1, tm, Q_DIM), lambda b, j: (b, j, OFF_Q // Q_DIM)),
            pl.BlockSpec((1, tm, Q_DIM), lambda b, j: (b, j, OFF_K // Q_DIM)),
            pl.BlockSpec((1, tm, V_DIM), lambda b, j: (b, j, OFF_V // V_DIM)),
            pl.BlockSpec((1, tm, V_DIM), lambda b, j: (b, j, OFF_G // V_DIM)),
            pl.BlockSpec((tm, RET_DK), lambda b, j: (j, 0)),
            pl.BlockSpec((tm, RET_DK), lambda b, j: (j, 0)),
            pl.BlockSpec((H, C, C), full3),
            pl.BlockSpec((H, C, RET_DK), full3),
            pl.BlockSpec((H, C, RET_DK), full3),
            pl.BlockSpec((H, 8, RET_DV), full3),
        ],
        out_specs=pl.BlockSpec((1, tm, V_DIM), lambda b, j: (b, j, 0)),
        out_shape=jax.ShapeDtypeStruct((B, S, V_DIM), BF16),
        scratch_shapes=[pltpu.VMEM((H, RET_DK, RET_DV), F32)],
        compiler_params=_cparams(2),
        name="retention",
    )(proj3, proj3, proj3, proj3, cos_full, sin_signed, dmask, xi_b, zeta_b, g_b)


def _pool_kernel(cur_ref, halo_ref, wl_ref, sc_ref, o_ref, *, tm):
    j = pl.program_id(1)
    cur = cur_ref[0]
    halo = jnp.where(j == 0, 0.0, halo_ref[0])
    ext = jnp.concatenate([halo, cur], axis=0)
    pos = j * tm + lax.broadcasted_iota(I32, (tm, 1), 0)
    for g, w in enumerate(POOL_WINDOWS):
        cols = slice(g * POOL_GDIM, (g + 1) * POOL_GDIM)
        s = ext[:, cols]
        sh = 1
        while sh < w:
            s = s + pltpu.roll(s, sh, 0)
            sh *= 2
        win = s[POOL_HALO:, :]
        count = jnp.minimum(pos + 1, w).astype(F32)
        pooled = win / count - cur[:, cols]
        lin = jnp.dot(pooled.astype(BF16), wl_ref[g], preferred_element_type=F32)
        o_ref[0, :, cols] = (lin * sc_ref[:, cols]).astype(BF16)


def _pool(proj3, wl_bf, pool_scale):
    B, S, _ = proj3.shape
    tm = 512
    return pl.pallas_call(
        functools.partial(_pool_kernel, tm=tm),
        grid=(B, S // tm),
        in_specs=[
            pl.BlockSpec((1, tm, POOL_DIM), lambda b, j: (b, j, OFF_P // POOL_DIM)),
            pl.BlockSpec((1, POOL_HALO, POOL_DIM),
                         lambda b, j: (b, jnp.maximum(j * (tm // POOL_HALO) - 1, 0), OFF_P // POOL_DIM)),
            pl.BlockSpec((POOL_GROUPS, POOL_GDIM, POOL_GDIM), lambda b, j: (0, 0, 0)),
            pl.BlockSpec((1, POOL_DIM), lambda b, j: (0, 0)),
        ],
        out_specs=pl.BlockSpec((1, tm, POOL_DIM), lambda b, j: (b, j, 0)),
        out_shape=jax.ShapeDtypeStruct((B, S, POOL_DIM), BF16),
        compiler_params=_cparams(2),
        name="pool",
    )(proj3, proj3, wl_bf, pool_scale.reshape(1, POOL_DIM))


def _mix_kernel(ret_ref, wr_ref, pool_ref, wp_ref, gr_ref, gp_ref, o_ref):
    y_ret = jnp.dot(ret_ref[...], wr_ref[...], preferred_element_type=F32)
    y_pool = jnp.dot(pool_ref[...], wp_ref[...], preferred_element_type=F32)
    mixed = jax.nn.sigmoid(gr_ref[...]) * y_ret + jax.nn.sigmoid(gp_ref[...]) * y_pool
    o_ref[...] = mixed.astype(BF16)


def _mix(ret2, wr_bf, pool2, wp_bf, proj2):
    T = ret2.shape[0]
    tm, tn = 1024, 512
    off_r = OFF_BG // tn
    off_p = (OFF_BG + D_MODEL) // tn
    return pl.pallas_call(
        _mix_kernel,
        grid=(T // tm, D_MODEL // tn),
        in_specs=[
            pl.BlockSpec((tm, V_DIM), lambda i, j: (i, 0)),
            pl.BlockSpec((V_DIM, tn), lambda i, j: (0, j)),
            pl.BlockSpec((tm, POOL_DIM), lambda i, j: (i, 0)),
            pl.BlockSpec((POOL_DIM, tn), lambda i, j: (0, j)),
            pl.BlockSpec((tm, tn), lambda i, j: (i, off_r + j)),
            pl.BlockSpec((tm, tn), lambda i, j: (i, off_p + j)),
        ],
        out_specs=pl.BlockSpec((tm, tn), lambda i, j: (i, j)),
        out_shape=jax.ShapeDtypeStruct((T, D_MODEL), BF16),
        compiler_params=_cparams(2),
        name="mix",
    )(ret2, wr_bf, pool2, wp_bf, proj2, proj2)


def _resid_kernel(m_ref, w_ref, x_ref, o_ref):
    o_ref[...] = x_ref[...] + jnp.dot(m_ref[...], w_ref[...], preferred_element_type=F32)


def _resid(mixed, w_bf, x2):
    T = x2.shape[0]
    tm, tn = 1024, 512
    return pl.pallas_call(
        _resid_kernel,
        grid=(T // tm, D_MODEL // tn),
        in_specs=[
            pl.BlockSpec((tm, D_MODEL), lambda i, j: (i, 0)),
            pl.BlockSpec((D_MODEL, tn), lambda i, j: (0, j)),
            pl.BlockSpec((tm, tn), lambda i, j: (i, j)),
        ],
        out_specs=pl.BlockSpec((tm, tn), lambda i, j: (i, j)),
        out_shape=jax.ShapeDtypeStruct((T, D_MODEL), F32),
        compiler_params=_cparams(2),
        name="resid",
    )(mixed, w_bf, x2)


def _topk_axis0(ss, k, iota, n):
    ss = list(ss)
    vals = [[] for _ in ss]
    idxs = [[] for _ in ss]
    for _ in range(k):
        for c, s in enumerate(ss):
            m = jnp.max(s, axis=0, keepdims=True)
            idx = jnp.min(jnp.where(s == m, iota, float(n)), axis=0, keepdims=True)
            vals[c].append(m)
            idxs[c].append(idx)
            ss[c] = jnp.where(iota == idx, -jnp.inf, s)
    return ([jnp.concatenate(v, axis=0) for v in vals], [jnp.concatenate(i, axis=0) for i in idxs])


PEERQ_LANES = LANES
SUBLANES = 8


def _peerq_pairs(k):
    pairs, i = [], 0
    while i < k and k // (i + 1) > 1:
        nj = -(-(k // (i + 1)) // SUBLANES) * SUBLANES
        pairs.append(((i, 1, nj), nj))
        i += 1
    assert (k - i) % SUBLANES == 0
    for i0 in range(i, k, SUBLANES):
        pairs.append(((i0, SUBLANES, 1), SUBLANES))
    return pairs


PEERQ_PAIRS = _peerq_pairs(PEER_TOPK)


def _peerq_kernel(x_ref, nw_ref, wq_ref, keys_ref, h_ref, e_ref, g_ref, q_scr, *, tm):
    x = x_ref[...]
    ms = jnp.mean(x * x, axis=-1, keepdims=True)
    hb = ((x * lax.rsqrt(ms + EPS)) * nw_ref[...]).astype(BF16)
    h_ref[...] = hb.astype(F32)
    q_scr[...] = jnp.dot(hb, wq_ref[...], preferred_element_type=F32).astype(BF16)

    K = PEER_TOPK
    tl = PEERQ_LANES
    n_cand = sum(n for _, n in PEERQ_PAIRS)
    iota_k = lax.broadcasted_iota(I32, (PEER_NKEYS, tl), 0).astype(F32)
    iota_c = lax.broadcasted_iota(I32, (n_cand, tl), 0).astype(F32)

    def head_tile(h, toks):
        scores = []
        for p in range(2):
            off = pl.multiple_of(h * PEER_DKEY + p * PEER_DHALF, PEER_DHALF)
            qhp = q_scr[toks, pl.ds(off, PEER_DHALF)]
            scores.append(lax.dot_general(keys_ref[p], qhp, (((1,), (1,)), ((), ())),
                                          preferred_element_type=F32))
        sv, si = _topk_axis0(scores, K, iota_k, PEER_NKEYS)
        cand = jnp.concatenate([sv[0][i0:i0 + ni, :] + sv[1][0:nj, :]
                                for (i0, ni, nj), _ in PEERQ_PAIRS], axis=0)
        cidx = jnp.concatenate([si[0][i0:i0 + ni, :] * PEER_NKEYS + si[1][0:nj, :]
                                for (i0, ni, nj), _ in PEERQ_PAIRS], axis=0)
        best, eidx = [], []
        for _ in range(K):
            m = jnp.max(cand, axis=0, keepdims=True)
            pos = jnp.min(jnp.where(cand == m, iota_c, float(n_cand)), axis=0, keepdims=True)
            hit = iota_c == pos
            best.append(m)
            eidx.append(jnp.max(jnp.where(hit, cidx, -1.0), axis=0, keepdims=True))
            cand = jnp.where(hit, -jnp.inf, cand)
        best = jnp.concatenate(best, axis=0)
        ex = jnp.exp(best - best[0:1, :])
        gate = ex / jnp.sum(ex, axis=0, keepdims=True)
        row = pl.multiple_of(h * K, K)
        e_ref[pl.ds(row, K), toks] = jnp.concatenate(eidx, axis=0).astype(I32)
        g_ref[pl.ds(row, K), toks] = gate

    def head(h, carry):
        for t0 in range(0, tm, tl):
            head_tile(h, slice(t0, t0 + tl))
        return carry

    lax.fori_loop(0, PEER_HEADS, head, 0)


def _peerq(x1, norm_w, wq_bf, keys_bf):
    T = x1.shape[0]
    tm = 256
    return pl.pallas_call(
        functools.partial(_peerq_kernel, tm=tm),
        grid=(T // tm,),
        in_specs=[
            pl.BlockSpec((tm, D_MODEL), lambda i: (i, 0)),
            pl.BlockSpec((1, D_MODEL), lambda i: (0, 0)),
            pl.BlockSpec((D_MODEL, PEER_HEADS * PEER_DKEY), lambda i: (0, 0)),
            pl.BlockSpec((2, PEER_NKEYS, PEER_DHALF), lambda i: (0, 0, 0)),
        ],
        out_specs=[
            pl.BlockSpec((tm, D_MODEL), lambda i: (i, 0)),
            pl.BlockSpec((PEER_SEL, tm), lambda i: (0, i)),
            pl.BlockSpec((PEER_SEL, tm), lambda i: (0, i)),
        ],
        out_shape=[
            jax.ShapeDtypeStruct((T, D_MODEL), F32),
            jax.ShapeDtypeStruct((PEER_SEL, T), I32),
            jax.ShapeDtypeStruct((PEER_SEL, T), F32),
        ],
        scratch_shapes=[pltpu.VMEM((tm, PEER_HEADS * PEER_DKEY), BF16)],
        compiler_params=_cparams(1),
        name="peerq",
    )(x1, norm_w.reshape(1, D_MODEL), wq_bf, keys_bf)


PEER_SLOTS = 8
PEER_AHEAD = 6
PEER_TB = 128
PEER_NEXT = 8
PEER_NACC = 4


def _peer_kernel(idx_ref, gate_ref, h_ref, hn_ref, tab_ref, y_ref, *scratch):
    bufs = scratch[:PEER_SLOTS]
    a_scr, sem = scratch[PEER_SLOTS:]

    def issue(tok, slot):
        for r in range(PEER_SEL):
            pltpu.make_async_copy(tab_ref.at[idx_ref[0, tok, r]], bufs[slot].at[r],
                                  sem.at[slot]).start(priority=r % 2)

    def wait(slot):
        pltpu.make_async_copy(tab_ref.at[pl.ds(0, PEER_SEL)], bufs[slot], sem.at[slot]).wait()

    lane = lax.broadcasted_iota(I32, (8, LANES), 1)

    sub = lax.broadcasted_iota(I32, (8, LANES), 0)

    def weights(tok, a_part):
        a_row = jnp.sum(a_part, axis=0, keepdims=True)
        act = 0.5 * a_row * (1.0 + lax.erf(a_row * np.float32(math.sqrt(0.5))))
        w_row = (act * gate_ref[pl.ds(tok, 1), :]).astype(BF16).astype(F32)
        return jnp.broadcast_to(w_row, (8, LANES))

    def rows(h, u_slot, v_tok, v_slot, a_prev):
        accs = [jnp.zeros((8, LANES), F32) for _ in range(PEER_NACC)]
        if v_tok is not None:
            w8 = weights(v_tok, a_prev)
            ys = [jnp.zeros((ROW_SUB, LANES), F32) for _ in range(PEER_NACC)]
        for r in range(PEER_SEL):
            c = r % PEER_NACC
            p = bufs[u_slot][r, 0:ROW_SUB, :].astype(F32) * h
            s = jnp.sum(p[0:8] + p[8:16], axis=1, keepdims=True)
            accs[c] = jnp.where(lane == r, s, accs[c])
            if v_tok is not None:
                if r % 8 == 0:
                    w_col = jnp.sum(jnp.where(lane == sub + r, w8, 0.0), axis=1, keepdims=True)
                ys[c] = ys[c] + w_col[r % 8:r % 8 + 1, :] * bufs[v_slot][r, ROW_SUB:2 * ROW_SUB, :].astype(F32)
        if v_tok is not None:
            y_ref[v_tok] = (ys[0] + ys[1]) + (ys[2] + ys[3])
        return (accs[0] + accs[1]) + (accs[2] + accs[3])

    step_i = pl.program_id(0)
    n_groups = PEER_TB // PEER_SLOTS

    @pl.when(step_i == 0)
    def _():
        for t in range(PEER_AHEAD + 1):
            issue(t, t)
        wait(0)
        a_scr[...] = rows(h_ref[0], 0, None, None, None)

    def group(g, a_prev):
        for k in range(PEER_SLOTS):
            j = g * PEER_SLOTS + k
            wait((k + 1) % PEER_SLOTS)
            issue(j + (PEER_AHEAD + 1), (k + PEER_AHEAD + 1) % PEER_SLOTS)
            if k + 1 < PEER_SLOTS:
                h = h_ref[j + 1]
            else:
                h = jnp.where(g == n_groups - 1, hn_ref[0], h_ref[jnp.minimum(j + 1, PEER_TB - 1)])
            a_prev = rows(h, (k + 1) % PEER_SLOTS, j, k, a_prev)
        return a_prev

    a_scr[...] = lax.fori_loop(0, n_groups, group, a_scr[...])

    @pl.when(step_i == pl.num_programs(0) - 1)
    def _():
        for t in range(PEER_TB + 1, PEER_TB + PEER_AHEAD + 1):
            wait(t % PEER_SLOTS)


def _peer(eidx, gate, h3, table):
    T = eidx.shape[0]
    tb = PEER_TB
    nb = T // tb
    idx3 = eidx.reshape(nb, tb, PEER_SEL)
    idx_ext = jnp.concatenate([idx3, jnp.roll(idx3[:, :PEER_NEXT], -1, axis=0)], axis=1)
    hn_blocks = T // PEER_NEXT
    return pl.pallas_call(
        _peer_kernel,
        grid=(nb,),
        in_specs=[
            pl.BlockSpec((1, tb + PEER_NEXT, PEER_SEL), lambda i: (i, 0, 0), memory_space=pltpu.SMEM),
            pl.BlockSpec((tb, PEER_SEL), lambda i: (i, 0)),
            pl.BlockSpec((tb, ROW_SUB, LANES), lambda i: (i, 0, 0)),
            pl.BlockSpec((PEER_NEXT, ROW_SUB, LANES),
                         lambda i: (jnp.minimum((i + 1) * (tb // PEER_NEXT), hn_blocks - 1), 0, 0)),
            pl.BlockSpec(memory_space=pl.ANY),
        ],
        out_specs=pl.BlockSpec((tb, ROW_SUB, LANES), lambda i: (i, 0, 0)),
        out_shape=jax.ShapeDtypeStruct((T, ROW_SUB, LANES), F32),
        scratch_shapes=(
            [pltpu.VMEM((PEER_SEL, 2 * ROW_SUB, LANES), BF16) for _ in range(PEER_SLOTS)]
            + [pltpu.VMEM((8, LANES), F32), pltpu.SemaphoreType.DMA((PEER_SLOTS,))]),
        compiler_params=_cparams(1),
        name="peer",
    )(idx_ext, gate, h3, h3, table)


def _pack_table(u, v):
    E = u.shape[0]
    both = jnp.stack([u.reshape(E, ROW_SUB, LANES), v.reshape(E, ROW_SUB, LANES)], axis=1)
    return both.astype(BF16).reshape(E, 2 * ROW_SUB, LANES)


def _final_kernel(x_ref, y_ref, nw_ref, o_ref):
    x = x_ref[...] + y_ref[...]
    ms = jnp.mean(x * x, axis=-1, keepdims=True)
    o_ref[...] = (x * lax.rsqrt(ms + EPS)) * nw_ref[...]


def _final(x1, y, norm_w):
    T = x1.shape[0]
    tm = 512
    return pl.pallas_call(
        _final_kernel,
        grid=(T // tm,),
        in_specs=[
            pl.BlockSpec((tm, D_MODEL), lambda i: (i, 0)),
            pl.BlockSpec((tm, D_MODEL), lambda i: (i, 0)),
            pl.BlockSpec((1, D_MODEL), lambda i: (0, 0)),
        ],
        out_specs=pl.BlockSpec((tm, D_MODEL), lambda i: (i, 0)),
        out_shape=jax.ShapeDtypeStruct((T, D_MODEL), F32),
        compiler_params=_cparams(1),
        name="final",
    )(x1, y, norm_w.reshape(1, D_MODEL))


def kernel(x, norm1_w, w_in, w_ret_o, w_pool_lin, pool_scale, w_pool_o, w_out,
           norm2_w, peer_w_q, peer_sub_keys, peer_u, peer_v, final_norm_w):
    B, S, D = x.shape
    T = B * S
    depth = w_in.shape[0]
    consts = _retention_constants(S)
    x2 = x.reshape(T, D)
    for l in range(depth):
        proj = _inproj(x2, norm1_w[l], w_in[l].astype(BF16))
        proj3 = proj.reshape(B, S, IN_COLS)
        ret = _retention(proj3, consts)
        pooled = _pool(proj3, w_pool_lin[l].astype(BF16), pool_scale[l])
        mixed = _mix(ret.reshape(T, V_DIM), w_ret_o[l].astype(BF16),
                     pooled.reshape(T, POOL_DIM), w_pool_o[l].astype(BF16), proj)
        x1 = _resid(mixed, w_out[l].astype(BF16), x2)
        h2, eidx_t, gate_t = _peerq(x1, norm2_w[l], peer_w_q[l].astype(BF16),
                                    peer_sub_keys[l].astype(BF16))
        table = _pack_table(peer_u[l], peer_v[l])
        y = _peer(eidx_t.T, gate_t.T, h2.reshape(T, ROW_SUB, LANES), table)
        x2 = x1 + y.reshape(T, D) if l + 1 < depth else x1
        y_last = y.reshape(T, D)
    out = _final(x2, y_last, final_norm_w)
    return out.reshape(B, S, D)
```

```python
import functools
import math

import numpy as np
import jax
import jax.numpy as jnp
from jax import lax
from jax.experimental import pallas as pl
from jax.experimental.pallas import tpu as pltpu

F32 = jnp.float32
BF16 = jnp.bfloat16
I32 = jnp.int32

D_MODEL = 2048
RET_HEADS = 8
RET_DK = 128
RET_DV = 256
RET_CHUNK = 128
POOL_WINDOWS = (2, 4, 8, 16)
POOL_GROUPS = len(POOL_WINDOWS)
POOL_DIM = 1024
POOL_GDIM = POOL_DIM // POOL_GROUPS
POOL_HALO = 16
PEER_HEADS = 8
PEER_NKEYS = 128
PEER_DKEY = 256
PEER_DHALF = PEER_DKEY // 2
PEER_TOPK = 16
PEER_SEL = PEER_HEADS * PEER_TOPK
ROPE_BASE = 10000.0
EPS = 1e-6

Q_DIM = RET_HEADS * RET_DK
V_DIM = RET_HEADS * RET_DV
OFF_Q = 0
OFF_K = OFF_Q + Q_DIM
OFF_V = OFF_K + Q_DIM
OFF_G = OFF_V + V_DIM
OFF_P = OFF_G + V_DIM
OFF_BG = OFF_P + POOL_DIM
IN_COLS = OFF_BG + 2 * D_MODEL

LANES = 128
ROW_SUB = D_MODEL // LANES
VMEM_LIMIT = 48 * 1024 * 1024


def _cparams(n_axes):
    return pltpu.CompilerParams(
        dimension_semantics=("arbitrary",) * n_axes, vmem_limit_bytes=VMEM_LIMIT)


def _inproj_kernel(x_ref, nw_ref, w_ref, o_ref, h_scr):
    @pl.when(pl.program_id(1) == 0)
    def _():
        x = x_ref[...]
        ms = jnp.mean(x * x, axis=-1, keepdims=True)
        h_scr[...] = ((x * lax.rsqrt(ms + EPS)) * nw_ref[...]).astype(BF16)

    o_ref[...] = jnp.dot(h_scr[...], w_ref[...], preferred_element_type=F32)


def _inproj(x2, norm_w, w_bf):
    T = x2.shape[0]
    tm, tn = 1024, 1024
    return pl.pallas_call(
        _inproj_kernel,
        grid=(T // tm, IN_COLS // tn),
        in_specs=[
            pl.BlockSpec((tm, D_MODEL), lambda i, j: (i, 0)),
            pl.BlockSpec((1, D_MODEL), lambda i, j: (0, 0)),
            pl.BlockSpec((D_MODEL, tn), lambda i, j: (0, j)),
        ],
        out_specs=pl.BlockSpec((tm, tn), lambda i, j: (i, j)),
        out_shape=jax.ShapeDtypeStruct((T, IN_COLS), F32),
        scratch_shapes=[pltpu.VMEM((tm, D_MODEL), BF16)],
        compiler_params=_cparams(2),
        name="inproj",
    )(x2, norm_w.reshape(1, D_MODEL), w_bf)


def _retention_constants(seq):
    H, C = RET_HEADS, RET_CHUNK
    half = RET_DK // 2
    inv_freq = ROPE_BASE ** (-jnp.arange(half, dtype=F32) / half)
    ang = jnp.arange(seq, dtype=F32)[:, None] * inv_freq[None, :]
    cos, sin = jnp.cos(ang), jnp.sin(ang)
    cos_full = jnp.concatenate([cos, cos], axis=-1)
    sin_signed = jnp.concatenate([-sin, sin], axis=-1)
    log_g = jnp.log(1.0 - 2.0 ** (-5.0 - jnp.arange(H, dtype=F32)))
    i = jnp.arange(C, dtype=F32)
    diff = i[:, None] - i[None, :]
    dmask = jnp.where(diff[None] >= 0,
                      jnp.exp(log_g[:, None, None] * jnp.maximum(diff, 0.0)[None]), 0.0)
    xi = jnp.exp(log_g[:, None] * (i[None, :] + 1.0))
    zeta = jnp.exp(log_g[:, None] * (C - 1.0 - i)[None, :])
    g_chunk = jnp.exp(log_g * C)
    xi_b = jnp.broadcast_to(xi[:, :, None], (H, C, RET_DK))
    zeta_b = jnp.broadcast_to(zeta[:, :, None], (H, C, RET_DK))
    g_b = jnp.broadcast_to(g_chunk[:, None, None], (H, 8, RET_DV))
    return cos_full, sin_signed, dmask, xi_b, zeta_b, g_b


def _ret_kernel(q_ref, k_ref, v_ref, g_ref, cos_ref, sin_ref, dm_ref, xi_ref, ze_ref, gc_ref,
                o_ref, r_scr, *, n_chunks):
    @pl.when(pl.program_id(1) == 0)
    def _():
        r_scr[...] = jnp.zeros_like(r_scr)

    scale = RET_DK ** -0.5
    for c in range(n_chunks):
        rows = slice(c * RET_CHUNK, (c + 1) * RET_CHUNK)
        cos = cos_ref[rows, :]
        sin = sin_ref[rows, :]
        for h in range(RET_HEADS):
            kc = slice(h * RET_DK, (h + 1) * RET_DK)
            vc = slice(h * RET_DV, (h + 1) * RET_DV)
            q = q_ref[0, rows, kc]
            k = k_ref[0, rows, kc]
            qr = q * cos + pltpu.roll(q, RET_DK // 2, 1) * sin
            kr = (k * cos + pltpu.roll(k, RET_DK // 2, 1) * sin) * scale
            vb = v_ref[0, rows, vc].astype(BF16)
            sc = lax.dot_general(qr.astype(BF16), kr.astype(BF16), (((1,), (1,)), ((), ())),
                                 preferred_element_type=F32) * dm_ref[h]
            inner = jnp.dot(sc.astype(BF16), vb, preferred_element_type=F32)
            state = r_scr[h]
            cross = jnp.dot((qr * xi_ref[h]).astype(BF16), state.astype(BF16),
                            preferred_element_type=F32)
            kzt = (kr * ze_ref[h]).T.astype(BF16)
            r_scr[h] = gc_ref[h, 0:1, :] * state + jnp.dot(kzt, vb, preferred_element_type=F32)
            o = inner + cross
            mu = jnp.mean(o, axis=-1, keepdims=True)
            var = jnp.mean((o - mu) ** 2, axis=-1, keepdims=True)
            y = (o - mu) * lax.rsqrt(var + EPS)
            g = g_ref[0, rows, vc]
            o_ref[0, rows, vc] = ((g * jax.nn.sigmoid(g)) * y).astype(BF16)


def _retention(proj3, consts):
    B, S, _ = proj3.shape
    n_chunks = 2
    tm = n_chunks * RET_CHUNK
    cos_full, sin_signed, dmask, xi_b, zeta_b, g_b = consts
    H, C = RET_HEADS, RET_CHUNK
    full3 = lambda b, j: (0, 0, 0)
    return pl.pallas_call(
        functools.partial(_ret_kernel, n_chunks=n_chunks),
        grid=(B, S // tm),
        in_specs=[
            pl.BlockSpec((1, tm, Q_DIM), lambda b, j: (b, j, OFF_Q // Q_DIM)),
            pl.BlockSpec((1, tm, Q_DIM), lambda b, j: (b, j, OFF_K // Q_DIM)),
            pl.BlockSpec((1, tm, V_DIM), lambda b, j: (b, j, OFF_V // V_DIM)),
            pl.BlockSpec((1, tm, V_DIM), lambda b, j: (b, j, OFF_G // V_DIM)),
            pl.BlockSpec((tm, RET_DK), lambda b, j: (j, 0)),
            pl.BlockSpec((tm, RET_DK), lambda b, j: (j, 0)),
            pl.BlockSpec((H, C, C), full3),
            pl.BlockSpec((H, C, RET_DK), full3),
            pl.BlockSpec((H, C, RET_DK), full3),
            pl.BlockSpec((H, 8, RET_DV), full3),
        ],
        out_specs=pl.BlockSpec((1, tm, V_DIM), lambda b, j: (b, j, 0)),
        out_shape=jax.ShapeDtypeStruct((B, S, V_DIM), BF16),
        scratch_shapes=[pltpu.VMEM((H, RET_DK, RET_DV), F32)],
        compiler_params=_cparams(2),
        name="retention",
    )(proj3, proj3, proj3, proj3, cos_full, sin_signed, dmask, xi_b, zeta_b, g_b)


def _pool_kernel(cur_ref, halo_ref, wl_ref, sc_ref, o_ref, *, tm):
    j = pl.program_id(1)
    cur = cur_ref[0]
    halo = jnp.where(j == 0, 0.0, halo_ref[0])
    ext = jnp.concatenate([halo, cur], axis=0)
    pos = j * tm + lax.broadcasted_iota(I32, (tm, 1), 0)
    for g, w in enumerate(POOL_WINDOWS):
        cols = slice(g * POOL_GDIM, (g + 1) * POOL_GDIM)
        s = ext[:, cols]
        sh = 1
        while sh < w:
            s = s + pltpu.roll(s, sh, 0)
            sh *= 2
        win = s[POOL_HALO:, :]
        count = jnp.minimum(pos + 1, w).astype(F32)
        pooled = win / count - cur[:, cols]
        lin = jnp.dot(pooled.astype(BF16), wl_ref[g], preferred_element_type=F32)
        o_ref[0, :, cols] = (lin * sc_ref[:, cols]).astype(BF16)


def _pool(proj3, wl_bf, pool_scale):
    B, S, _ = proj3.shape
    tm = 512
    return pl.pallas_call(
        functools.partial(_pool_kernel, tm=tm),
        grid=(B, S // tm),
        in_specs=[
            pl.BlockSpec((1, tm, POOL_DIM), lambda b, j: (b, j, OFF_P // POOL_DIM)),
            pl.BlockSpec((1, POOL_HALO, POOL_DIM),
                         lambda b, j: (b, jnp.maximum(j * (tm // POOL_HALO) - 1, 0), OFF_P // POOL_DIM)),
            pl.BlockSpec((POOL_GROUPS, POOL_GDIM, POOL_GDIM), lambda b, j: (0, 0, 0)),
            pl.BlockSpec((1, POOL_DIM), lambda b, j: (0, 0)),
        ],
        out_specs=pl.BlockSpec((1, tm, POOL_DIM), lambda b, j: (b, j, 0)),
        out_shape=jax.ShapeDtypeStruct((B, S, POOL_DIM), BF16),
        compiler_params=_cparams(2),
        name="pool",
    )(proj3, proj3, wl_bf, pool_scale.reshape(1, POOL_DIM))


def _mix_kernel(ret_ref, wr_ref, pool_ref, wp_ref, gr_ref, gp_ref, o_ref):
    y_ret = jnp.dot(ret_ref[...], wr_ref[...], preferred_element_type=F32)
    y_pool = jnp.dot(pool_ref[...], wp_ref[...], preferred_element_type=F32)
    mixed = jax.nn.sigmoid(gr_ref[...]) * y_ret + jax.nn.sigmoid(gp_ref[...]) * y_pool
    o_ref[...] = mixed.astype(BF16)


def _mix(ret2, wr_bf, pool2, wp_bf, proj2):
    T = ret2.shape[0]
    tm, tn = 1024, 512
    off_r = OFF_BG // tn
    off_p = (OFF_BG + D_MODEL) // tn
    return pl.pallas_call(
        _mix_kernel,
        grid=(T // tm, D_MODEL // tn),
        in_specs=[
            pl.BlockSpec((tm, V_DIM), lambda i, j: (i, 0)),
            pl.BlockSpec((V_DIM, tn), lambda i, j: (0, j)),
            pl.BlockSpec((tm, POOL_DIM), lambda i, j: (i, 0)),
            pl.BlockSpec((POOL_DIM, tn), lambda i, j: (0, j)),
            pl.BlockSpec((tm, tn), lambda i, j: (i, off_r + j)),
            pl.BlockSpec((tm, tn), lambda i, j: (i, off_p + j)),
        ],
        out_specs=pl.BlockSpec((tm, tn), lambda i, j: (i, j)),
        out_shape=jax.ShapeDtypeStruct((T, D_MODEL), BF16),
        compiler_params=_cparams(2),
        name="mix",
    )(ret2, wr_bf, pool2, wp_bf, proj2, proj2)


def _resid_kernel(m_ref, w_ref, x_ref, o_ref):
    o_ref[...] = x_ref[...] + jnp.dot(m_ref[...], w_ref[...], preferred_element_type=F32)


def _resid(mixed, w_bf, x2):
    T = x2.shape[0]
    tm, tn = 1024, 512
    return pl.pallas_call(
        _resid_kernel,
        grid=(T // tm, D_MODEL // tn),
        in_specs=[
            pl.BlockSpec((tm, D_MODEL), lambda i, j: (i, 0)),
            pl.BlockSpec((D_MODEL, tn), lambda i, j: (0, j)),
            pl.BlockSpec((tm, tn), lambda i, j: (i, j)),
        ],
        out_specs=pl.BlockSpec((tm, tn), lambda i, j: (i, j)),
        out_shape=jax.ShapeDtypeStruct((T, D_MODEL), F32),
        compiler_params=_cparams(2),
        name="resid",
    )(mixed, w_bf, x2)


class _Extraction:
    def __init__(self, s, ids, none_id):
        self.s, self.ids, self.none_id = s, ids, none_id
        self.vals, self.sel = [], []

    def step(self):
        m = jnp.max(self.s, axis=0, keepdims=True)
        pick = jnp.min(jnp.where(self.s == m, self.ids, self.none_id), axis=0, keepdims=True)
        hit = self.ids == pick
        self.vals.append(m)
        self.sel.append(pick)
        self.s = jnp.where(hit, -jnp.inf, self.s)
        return hit

    def values(self):
        return jnp.concatenate(self.vals, axis=0)

    def picks(self):
        return jnp.concatenate(self.sel, axis=0)


PEERQ_LANES = LANES
PEERQ_HEADS_PER_ITER = 4
SUBLANES = 8


def _peerq_pairs(k):
    singles, i = [], 0
    while i < k and k // (i + 1) > 1:
        singles.append((i, 1, k // (i + 1)))
        i += 1
    assert (k - i) % SUBLANES == 0
    groups = []
    for piece in sorted(singles, key=lambda p: -p[2]):
        for g in groups:
            if sum(p[2] for p in g) % SUBLANES and sum(p[2] for p in g) % SUBLANES + piece[2] <= SUBLANES:
                g.append(piece)
                break
        else:
            groups.append([piece])
    groups = [g + [None] * (-sum(p[2] for p in g) % SUBLANES) for g in groups]
    groups += [[(i0, SUBLANES, 1)] for i0 in range(i, k, SUBLANES)]
    return groups


PEERQ_GROUPS = _peerq_pairs(PEER_TOPK)


def _peerq_kernel(x_ref, nw_ref, wq_ref, keys_ref, h_ref, e_ref, g_ref, q_scr, *, tm):
    x = x_ref[...]
    ms = jnp.mean(x * x, axis=-1, keepdims=True)
    hb = ((x * lax.rsqrt(ms + EPS)) * nw_ref[...]).astype(BF16)
    h_ref[...] = hb.astype(F32)
    q_scr[...] = jnp.dot(hb, wq_ref[...], preferred_element_type=F32).astype(BF16)

    K = PEER_TOPK
    tl = PEERQ_LANES
    iota_k = lax.broadcasted_iota(I32, (PEER_NKEYS, tl), 0).astype(F32)
    pieces = [p for g in PEERQ_GROUPS for p in g]
    no_pair = float(K * K)

    def table(of_piece, pad):
        rows = [jnp.full((1, tl), pad, F32) if p is None else of_piece(*p) for p in pieces]
        return jnp.concatenate(rows, axis=0)

    def flat_of(i0, ni, nj):
        i = i0 + lax.broadcasted_iota(I32, (ni, tl), 0)
        j = lax.broadcasted_iota(I32, (nj, tl), 0)
        return (i * K + j).astype(F32)

    flat = table(flat_of, no_pair)

    def key_stage(h, toks):
        ex = []
        for p in range(2):
            off = pl.multiple_of(h * PEER_DKEY + p * PEER_DHALF, PEER_DHALF)
            qhp = q_scr[toks, pl.ds(off, PEER_DHALF)]
            s = lax.dot_general(keys_ref[p], qhp, (((1,), (1,)), ((), ())),
                                preferred_element_type=F32)
            ex.append(_Extraction(s, iota_k, float(PEER_NKEYS)))
        return ex

    def pair_stage(keys):
        sv = [e.values() for e in keys]
        si = [e.picks() for e in keys]
        cand = table(lambda i0, ni, nj: sv[0][i0:i0 + ni, :] + sv[1][0:nj, :], -jnp.inf)
        cidx = table(lambda i0, ni, nj: si[0][i0:i0 + ni, :] * PEER_NKEYS + si[1][0:nj, :], -1.0)
        return _Extraction(cand, flat, no_pair), cidx

    def heads(hb, carry):
        problems = [(hb * PEERQ_HEADS_PER_ITER + dh, slice(t0, t0 + tl))
                    for dh in range(PEERQ_HEADS_PER_ITER) for t0 in range(0, tm, tl)]
        pairs = None
        for n in range(len(problems) + 1):
            keys = key_stage(*problems[n]) if n < len(problems) else None
            eidx = []
            for _ in range(K):
                if keys is not None:
                    for e in keys:
                        e.step()
                if pairs is not None:
                    hit = pairs[0].step()
                    eidx.append(jnp.max(jnp.where(hit, pairs[1], -1.0), axis=0, keepdims=True))
            if pairs is not None:
                h, toks = problems[n - 1]
                best = pairs[0].values()
                ex = jnp.exp(best - best[0:1, :])
                row = pl.multiple_of(h * K, K)
                e_ref[pl.ds(row, K), toks] = jnp.concatenate(eidx, axis=0).astype(I32)
                g_ref[pl.ds(row, K), toks] = ex / jnp.sum(ex, axis=0, keepdims=True)
            pairs = pair_stage(keys) if keys is not None else None
        return carry

    lax.fori_loop(0, PEER_HEADS // PEERQ_HEADS_PER_ITER, heads, 0)


def _peerq(x1, norm_w, wq_bf, keys_bf):
    T = x1.shape[0]
    tm = 256
    return pl.pallas_call(
        functools.partial(_peerq_kernel, tm=tm),
        grid=(T // tm,),
        in_specs=[
            pl.BlockSpec((tm, D_MODEL), lambda i: (i, 0)),
            pl.BlockSpec((1, D_MODEL), lambda i: (0, 0)),
            pl.BlockSpec((D_MODEL, PEER_HEADS * PEER_DKEY), lambda i: (0, 0)),
            pl.BlockSpec((2, PEER_NKEYS, PEER_DHALF), lambda i: (0, 0, 0)),
        ],
        out_specs=[
            pl.BlockSpec((tm, D_MODEL), lambda i: (i, 0)),
            pl.BlockSpec((PEER_SEL, tm), lambda i: (0, i)),
            pl.BlockSpec((PEER_SEL, tm), lambda i: (0, i)),
        ],
        out_shape=[
            jax.ShapeDtypeStruct((T, D_MODEL), F32),
            jax.ShapeDtypeStruct((PEER_SEL, T), I32),
            jax.ShapeDtypeStruct((PEER_SEL, T), F32),
        ],
        scratch_shapes=[pltpu.VMEM((tm, PEER_HEADS * PEER_DKEY), BF16)],
        compiler_params=_cparams(1),
        name="peerq",
    )(x1, norm_w.reshape(1, D_MODEL), wq_bf, keys_bf)


PEER_SLOTS = 8
PEER_AHEAD = 6
PEER_TB = 128
PEER_NEXT = 8
PEER_NACC = 4


def _tree_sum(xs):
    while len(xs) > 1:
        xs = [a + b for a, b in zip(xs[0::2], xs[1::2])] + ([xs[-1]] if len(xs) % 2 else [])
    return xs[0]


def _peer_kernel(idx_ref, gate_ref, h_ref, hn_ref, tab_ref, y_ref, *scratch):
    bufs = scratch[:PEER_SLOTS]
    a_scr, sem = scratch[PEER_SLOTS:]

    def issue(tok, slot):
        for r in range(PEER_SEL):
            pltpu.make_async_copy(tab_ref.at[idx_ref[0, tok, r]], bufs[slot].at[r],
                                  sem.at[slot]).start(priority=r % 2)

    def wait(slot):
        pltpu.make_async_copy(tab_ref.at[pl.ds(0, PEER_SEL)], bufs[slot], sem.at[slot]).wait()

    lane = lax.broadcasted_iota(I32, (8, LANES), 1)

    sub = lax.broadcasted_iota(I32, (8, LANES), 0)

    def weights(tok, a_part):
        a_row = jnp.sum(a_part, axis=0, keepdims=True)
        act = 0.5 * a_row * (1.0 + lax.erf(a_row * np.float32(math.sqrt(0.5))))
        w_row = (act * gate_ref[pl.ds(tok, 1), :]).astype(BF16).astype(F32)
        return jnp.broadcast_to(w_row, (8, LANES))

    def rows(h, u_slot, v_tok, v_slot, a_prev):
        accs = [jnp.zeros((8, LANES), F32) for _ in range(PEER_NACC)]
        if v_tok is not None:
            w8 = weights(v_tok, a_prev)
            ys = [jnp.zeros((ROW_SUB, LANES), F32) for _ in range(PEER_NACC)]
        for r in range(PEER_SEL):
            c = r % PEER_NACC
            p = bufs[u_slot][r, 0:ROW_SUB, :].astype(F32) * h
            s = jnp.sum(p[0:8] + p[8:16], axis=1, keepdims=True)
            accs[c] = jnp.where(lane == r, s, accs[c])
            if v_tok is not None:
                if r % 8 == 0:
                    w_col = jnp.sum(jnp.where(lane == sub + r, w8, 0.0), axis=1, keepdims=True)
                ys[c] = ys[c] + w_col[r % 8:r % 8 + 1, :] * bufs[v_slot][r, ROW_SUB:2 * ROW_SUB, :].astype(F32)
        if v_tok is not None:
            y_ref[v_tok] = _tree_sum(ys)
        return _tree_sum(accs)

    step_i = pl.program_id(0)
    n_groups = PEER_TB // PEER_SLOTS

    @pl.when(step_i == 0)
    def _():
        for t in range(PEER_AHEAD + 1):
            issue(t, t)
        wait(0)
        a_scr[...] = rows(h_ref[0], 0, None, None, None)

    def group(g, a_prev):
        for k in range(PEER_SLOTS):
            j = g * PEER_SLOTS + k
            wait((k + 1) % PEER_SLOTS)
            issue(j + (PEER_AHEAD + 1), (k + PEER_AHEAD + 1) % PEER_SLOTS)
            if k + 1 < PEER_SLOTS:
                h = h_ref[j + 1]
            else:
                h = jnp.where(g == n_groups - 1, hn_ref[0], h_ref[jnp.minimum(j + 1, PEER_TB - 1)])
            a_prev = rows(h, (k + 1) % PEER_SLOTS, j, k, a_prev)
        return a_prev

    a_scr[...] = lax.fori_loop(0, n_groups, group, a_scr[...])

    @pl.when(step_i == pl.num_programs(0) - 1)
    def _():
        for t in range(PEER_TB + 1, PEER_TB + PEER_AHEAD + 1):
            wait(t % PEER_SLOTS)


def _peer(eidx, gate, h3, table):
    T = eidx.shape[0]
    tb = PEER_TB
    nb = T // tb
    idx3 = eidx.reshape(nb, tb, PEER_SEL)
    idx_ext = jnp.concatenate([idx3, jnp.roll(idx3[:, :PEER_NEXT], -1, axis=0)], axis=1)
    hn_blocks = T // PEER_NEXT
    return pl.pallas_call(
        _peer_kernel,
        grid=(nb,),
        in_specs=[
            pl.BlockSpec((1, tb + PEER_NEXT, PEER_SEL), lambda i: (i, 0, 0), memory_space=pltpu.SMEM),
            pl.BlockSpec((tb, PEER_SEL), lambda i: (i, 0)),
            pl.BlockSpec((tb, ROW_SUB, LANES), lambda i: (i, 0, 0)),
            pl.BlockSpec((PEER_NEXT, ROW_SUB, LANES),
                         lambda i: (jnp.minimum((i + 1) * (tb // PEER_NEXT), hn_blocks - 1), 0, 0)),
            pl.BlockSpec(memory_space=pl.ANY),
        ],
        out_specs=pl.BlockSpec((tb, ROW_SUB, LANES), lambda i: (i, 0, 0)),
        out_shape=jax.ShapeDtypeStruct((T, ROW_SUB, LANES), F32),
        scratch_shapes=(
            [pltpu.VMEM((PEER_SEL, 2 * ROW_SUB, LANES), BF16) for _ in range(PEER_SLOTS)]
            + [pltpu.VMEM((8, LANES), F32), pltpu.SemaphoreType.DMA((PEER_SLOTS,))]),
        compiler_params=_cparams(1),
        name="peer",
    )(idx_ext, gate, h3, h3, table)


def _pack_table(u, v):
    E = u.shape[0]
    both = jnp.stack([u.reshape(E, ROW_SUB, LANES), v.reshape(E, ROW_SUB, LANES)], axis=1)
    return both.astype(BF16).reshape(E, 2 * ROW_SUB, LANES)


def _final_kernel(x_ref, y_ref, nw_ref, o_ref):
    x = x_ref[...] + y_ref[...]
    ms = jnp.mean(x * x, axis=-1, keepdims=True)
    o_ref[...] = (x * lax.rsqrt(ms + EPS)) * nw_ref[...]


def _final(x1, y, norm_w):
    T = x1.shape[0]
    tm = 512
    return pl.pallas_call(
        _final_kernel,
        grid=(T // tm,),
        in_specs=[
            pl.BlockSpec((tm, D_MODEL), lambda i: (i, 0)),
            pl.BlockSpec((tm, D_MODEL), lambda i: (i, 0)),
            pl.BlockSpec((1, D_MODEL), lambda i: (0, 0)),
        ],
        out_specs=pl.BlockSpec((tm, D_MODEL), lambda i: (i, 0)),
        out_shape=jax.ShapeDtypeStruct((T, D_MODEL), F32),
        compiler_params=_cparams(1),
        name="final",
    )(x1, y, norm_w.reshape(1, D_MODEL))


def kernel(x, norm1_w, w_in, w_ret_o, w_pool_lin, pool_scale, w_pool_o, w_out,
           norm2_w, peer_w_q, peer_sub_keys, peer_u, peer_v, final_norm_w):
    B, S, D = x.shape
    T = B * S
    assert D == D_MODEL and w_in.shape == (1, D_MODEL, IN_COLS), "one layer of the stated widths"
    x2 = x.reshape(T, D)
    proj = _inproj(x2, norm1_w[0], w_in[0].astype(BF16))
    proj3 = proj.reshape(B, S, IN_COLS)
    ret = _retention(proj3, _retention_constants(S))
    pooled = _pool(proj3, w_pool_lin[0].astype(BF16), pool_scale[0])
    mixed = _mix(ret.reshape(T, V_DIM), w_ret_o[0].astype(BF16),
                 pooled.reshape(T, POOL_DIM), w_pool_o[0].astype(BF16), proj)
    x1 = _resid(mixed, w_out[0].astype(BF16), x2)
    h2, eidx_t, gate_t = _peerq(x1, norm2_w[0], peer_w_q[0].astype(BF16),
                                peer_sub_keys[0].astype(BF16))
    table = _pack_table(peer_u[0], peer_v[0])
    y = _peer(eidx_t.T, gate_t.T, h2.reshape(T, ROW_SUB, LANES), table)
    out = _final(x1, y.reshape(T, D), final_norm_w)
    return out.reshape(B, S, D)
```

```python
import functools
import math

import numpy as np
import jax
import jax.numpy as jnp
from jax import lax
from jax.experimental import pallas as pl
from jax.experimental.pallas import tpu as pltpu

F32 = jnp.float32
BF16 = jnp.bfloat16
I32 = jnp.int32

D_MODEL = 2048
RET_HEADS = 8
RET_DK = 128
RET_DV = 256
RET_CHUNK = 128
POOL_WINDOWS = (2, 4, 8, 16)
POOL_GROUPS = len(POOL_WINDOWS)
POOL_DIM = 1024
POOL_GDIM = POOL_DIM // POOL_GROUPS
POOL_HALO = 16
PEER_HEADS = 8
PEER_NKEYS = 128
PEER_DKEY = 256
PEER_DHALF = PEER_DKEY // 2
PEER_TOPK = 16
PEER_SEL = PEER_HEADS * PEER_TOPK
ROPE_BASE = 10000.0
EPS = 1e-6

Q_DIM = RET_HEADS * RET_DK
V_DIM = RET_HEADS * RET_DV
OFF_Q = 0
OFF_K = OFF_Q + Q_DIM
OFF_V = OFF_K + Q_DIM
OFF_G = OFF_V + V_DIM
OFF_P = OFF_G + V_DIM
OFF_BG = OFF_P + POOL_DIM
IN_COLS = OFF_BG + 2 * D_MODEL

LANES = 128
SUBLANES = 8
ROW_SUB = D_MODEL // LANES
assert ROW_SUB == 2 * SUBLANES
VMEM_LIMIT = 48 * 1024 * 1024


def _cparams(n_axes):
    return pltpu.CompilerParams(
        dimension_semantics=("arbitrary",) * n_axes, vmem_limit_bytes=VMEM_LIMIT)


def _inproj_kernel(x_ref, nw_ref, w_ref, o_ref, h_scr):
    @pl.when(pl.program_id(1) == 0)
    def _():
        x = x_ref[...]
        ms = jnp.mean(x * x, axis=-1, keepdims=True)
        h_scr[...] = ((x * lax.rsqrt(ms + EPS)) * nw_ref[...]).astype(BF16)

    o_ref[...] = jnp.dot(h_scr[...], w_ref[...], preferred_element_type=F32)


def _inproj(x2, norm_w, w_bf):
    T = x2.shape[0]
    tm, tn = 1024, 1024
    return pl.pallas_call(
        _inproj_kernel,
        grid=(T // tm, IN_COLS // tn),
        in_specs=[
            pl.BlockSpec((tm, D_MODEL), lambda i, j: (i, 0)),
            pl.BlockSpec((1, D_MODEL), lambda i, j: (0, 0)),
            pl.BlockSpec((D_MODEL, tn), lambda i, j: (0, j)),
        ],
        out_specs=pl.BlockSpec((tm, tn), lambda i, j: (i, j)),
        out_shape=jax.ShapeDtypeStruct((T, IN_COLS), F32),
        scratch_shapes=[pltpu.VMEM((tm, D_MODEL), BF16)],
        compiler_params=_cparams(2),
        name="inproj",
    )(x2, norm_w.reshape(1, D_MODEL), w_bf)


def _retention_constants(seq):
    H, C = RET_HEADS, RET_CHUNK
    half = RET_DK // 2
    inv_freq = ROPE_BASE ** (-jnp.arange(half, dtype=F32) / half)
    ang = jnp.arange(seq, dtype=F32)[:, None] * inv_freq[None, :]
    cos, sin = jnp.cos(ang), jnp.sin(ang)
    cos_full = jnp.concatenate([cos, cos], axis=-1)
    sin_signed = jnp.concatenate([-sin, sin], axis=-1)
    log_g = jnp.log(1.0 - 2.0 ** (-5.0 - jnp.arange(H, dtype=F32)))
    i = jnp.arange(C, dtype=F32)
    diff = i[:, None] - i[None, :]
    dmask = jnp.where(diff[None] >= 0,
                      jnp.exp(log_g[:, None, None] * jnp.maximum(diff, 0.0)[None]), 0.0)
    xi = jnp.exp(log_g[:, None] * (i[None, :] + 1.0))
    zeta = jnp.exp(log_g[:, None] * (C - 1.0 - i)[None, :])
    g_chunk = jnp.exp(log_g * C)
    xi_b = jnp.broadcast_to(xi[:, :, None], (H, C, RET_DK))
    zeta_b = jnp.broadcast_to(zeta[:, :, None], (H, C, RET_DK))
    g_b = jnp.broadcast_to(g_chunk[:, None, None], (H, SUBLANES, RET_DV))
    return cos_full, sin_signed, dmask, xi_b, zeta_b, g_b


def _ret_kernel(q_ref, k_ref, v_ref, g_ref, cos_ref, sin_ref, dm_ref, xi_ref, ze_ref, gc_ref,
                o_ref, r_scr, *, n_chunks):
    @pl.when(pl.program_id(1) == 0)
    def _():
        r_scr[...] = jnp.zeros_like(r_scr)

    scale = RET_DK ** -0.5
    for c in range(n_chunks):
        rows = slice(c * RET_CHUNK, (c + 1) * RET_CHUNK)
        cos = cos_ref[rows, :]
        sin = sin_ref[rows, :]
        for h in range(RET_HEADS):
            kc = slice(h * RET_DK, (h + 1) * RET_DK)
            vc = slice(h * RET_DV, (h + 1) * RET_DV)
            q = q_ref[0, rows, kc]
            k = k_ref[0, rows, kc]
            qr = q * cos + pltpu.roll(q, RET_DK // 2, 1) * sin
            kr = (k * cos + pltpu.roll(k, RET_DK // 2, 1) * sin) * scale
            vb = v_ref[0, rows, vc].astype(BF16)
            sc = lax.dot_general(qr.astype(BF16), kr.astype(BF16), (((1,), (1,)), ((), ())),
                                 preferred_element_type=F32) * dm_ref[h]
            inner = jnp.dot(sc.astype(BF16), vb, preferred_element_type=F32)
            state = r_scr[h]
            cross = jnp.dot((qr * xi_ref[h]).astype(BF16), state.astype(BF16),
                            preferred_element_type=F32)
            kzt = (kr * ze_ref[h]).T.astype(BF16)
            r_scr[h] = gc_ref[h, 0:1, :] * state + jnp.dot(kzt, vb, preferred_element_type=F32)
            o = inner + cross
            mu = jnp.mean(o, axis=-1, keepdims=True)
            var = jnp.mean((o - mu) ** 2, axis=-1, keepdims=True)
            y = (o - mu) * lax.rsqrt(var + EPS)
            g = g_ref[0, rows, vc]
            o_ref[0, rows, vc] = ((g * jax.nn.sigmoid(g)) * y).astype(BF16)


def _retention(proj3, consts):
    B, S, _ = proj3.shape
    n_chunks = 2
    tm = n_chunks * RET_CHUNK
    cos_full, sin_signed, dmask, xi_b, zeta_b, g_b = consts
    H, C = RET_HEADS, RET_CHUNK
    full3 = lambda b, j: (0, 0, 0)
    return pl.pallas_call(
        functools.partial(_ret_kernel, n_chunks=n_chunks),
        grid=(B, S // tm),
        in_specs=[
            pl.BlockSpec((1, tm, Q_DIM), lambda b, j: (b, j, OFF_Q // Q_DIM)),
            pl.BlockSpec((1, tm, Q_DIM), lambda b, j: (b, j, OFF_K // Q_DIM)),
            pl.BlockSpec((1, tm, V_DIM), lambda b, j: (b, j, OFF_V // V_DIM)),
            pl.BlockSpec((1, tm, V_DIM), lambda b, j: (b, j, OFF_G // V_DIM)),
            pl.BlockSpec((tm, RET_DK), lambda b, j: (j, 0)),
            pl.BlockSpec((tm, RET_DK), lambda b, j: (j, 0)),
            pl.BlockSpec((H, C, C), full3),
            pl.BlockSpec((H, C, RET_DK), full3),
            pl.BlockSpec((H, C, RET_DK), full3),
            pl.BlockSpec((H, SUBLANES, RET_DV), full3),
        ],
        out_specs=pl.BlockSpec((1, tm, V_DIM), lambda b, j: (b, j, 0)),
        out_shape=jax.ShapeDtypeStruct((B, S, V_DIM), BF16),
        scratch_shapes=[pltpu.VMEM((H, RET_DK, RET_DV), F32)],
        compiler_params=_cparams(2),
        name="retention",
    )(proj3, proj3, proj3, proj3, cos_full, sin_signed, dmask, xi_b, zeta_b, g_b)


def _pool_kernel(cur_ref, halo_ref, wl_ref, sc_ref, o_ref, *, tm):
    j = pl.program_id(1)
    cur = cur_ref[0]
    halo = jnp.where(j == 0, 0.0, halo_ref[0])
    ext = jnp.concatenate([halo, cur], axis=0)
    pos = j * tm + lax.broadcasted_iota(I32, (tm, 1), 0)
    for g, w in enumerate(POOL_WINDOWS):
        cols = slice(g * POOL_GDIM, (g + 1) * POOL_GDIM)
        s = ext[:, cols]
        sh = 1
        while sh < w:
            s = s + pltpu.roll(s, sh, 0)
            sh *= 2
        win = s[POOL_HALO:, :]
        count = jnp.minimum(pos + 1, w).astype(F32)
        pooled = win / count - cur[:, cols]
        lin = jnp.dot(pooled.astype(BF16), wl_ref[g], preferred_element_type=F32)
        o_ref[0, :, cols] = (lin * sc_ref[:, cols]).astype(BF16)


def _pool(proj3, wl_bf, pool_scale):
    B, S, _ = proj3.shape
    tm = 512
    return pl.pallas_call(
        functools.partial(_pool_kernel, tm=tm),
        grid=(B, S // tm),
        in_specs=[
            pl.BlockSpec((1, tm, POOL_DIM), lambda b, j: (b, j, OFF_P // POOL_DIM)),
            pl.BlockSpec((1, POOL_HALO, POOL_DIM),
                         lambda b, j: (b, jnp.maximum(j * (tm // POOL_HALO) - 1, 0), OFF_P // POOL_DIM)),
            pl.BlockSpec((POOL_GROUPS, POOL_GDIM, POOL_GDIM), lambda b, j: (0, 0, 0)),
            pl.BlockSpec((1, POOL_DIM), lambda b, j: (0, 0)),
        ],
        out_specs=pl.BlockSpec((1, tm, POOL_DIM), lambda b, j: (b, j, 0)),
        out_shape=jax.ShapeDtypeStruct((B, S, POOL_DIM), BF16),
        compiler_params=_cparams(2),
        name="pool",
    )(proj3, proj3, wl_bf, pool_scale.reshape(1, POOL_DIM))


def _mix_kernel(ret_ref, wr_ref, pool_ref, wp_ref, gr_ref, gp_ref, o_ref):
    y_ret = jnp.dot(ret_ref[...], wr_ref[...], preferred_element_type=F32)
    y_pool = jnp.dot(pool_ref[...], wp_ref[...], preferred_element_type=F32)
    mixed = jax.nn.sigmoid(gr_ref[...]) * y_ret + jax.nn.sigmoid(gp_ref[...]) * y_pool
    o_ref[...] = mixed.astype(BF16)


def _mix(ret2, wr_bf, pool2, wp_bf, proj2):
    T = ret2.shape[0]
    tm, tn = 1024, 512
    off_r = OFF_BG // tn
    off_p = (OFF_BG + D_MODEL) // tn
    return pl.pallas_call(
        _mix_kernel,
        grid=(T // tm, D_MODEL // tn),
        in_specs=[
            pl.BlockSpec((tm, V_DIM), lambda i, j: (i, 0)),
            pl.BlockSpec((V_DIM, tn), lambda i, j: (0, j)),
            pl.BlockSpec((tm, POOL_DIM), lambda i, j: (i, 0)),
            pl.BlockSpec((POOL_DIM, tn), lambda i, j: (0, j)),
            pl.BlockSpec((tm, tn), lambda i, j: (i, off_r + j)),
            pl.BlockSpec((tm, tn), lambda i, j: (i, off_p + j)),
        ],
        out_specs=pl.BlockSpec((tm, tn), lambda i, j: (i, j)),
        out_shape=jax.ShapeDtypeStruct((T, D_MODEL), BF16),
        compiler_params=_cparams(2),
        name="mix",
    )(ret2, wr_bf, pool2, wp_bf, proj2, proj2)


def _resid_kernel(m_ref, w_ref, x_ref, o_ref):
    o_ref[...] = x_ref[...] + jnp.dot(m_ref[...], w_ref[...], preferred_element_type=F32)


def _resid(mixed, w_bf, x2):
    T = x2.shape[0]
    tm, tn = 1024, 512
    return pl.pallas_call(
        _resid_kernel,
        grid=(T // tm, D_MODEL // tn),
        in_specs=[
            pl.BlockSpec((tm, D_MODEL), lambda i, j: (i, 0)),
            pl.BlockSpec((D_MODEL, tn), lambda i, j: (0, j)),
            pl.BlockSpec((tm, tn), lambda i, j: (i, j)),
        ],
        out_specs=pl.BlockSpec((tm, tn), lambda i, j: (i, j)),
        out_shape=jax.ShapeDtypeStruct((T, D_MODEL), F32),
        compiler_params=_cparams(2),
        name="resid",
    )(mixed, w_bf, x2)


class _Extraction:
    def __init__(self, s, ids, none_id):
        self.s, self.ids, self.none_id = s, ids, none_id
        self.vals, self.sel = [], []

    def step(self):
        m = jnp.max(self.s, axis=0, keepdims=True)
        pick = jnp.min(jnp.where(self.s == m, self.ids, self.none_id), axis=0, keepdims=True)
        hit = self.ids == pick
        self.vals.append(m)
        self.sel.append(pick)
        self.s = jnp.where(hit, -jnp.inf, self.s)
        return hit

    def values(self):
        return jnp.concatenate(self.vals, axis=0)

    def picks(self):
        return jnp.concatenate(self.sel, axis=0)


PEERQ_LANES = LANES
PEERQ_HEADS_PER_ITER = 4


def _peerq_pairs(k):
    singles, i = [], 0
    while i < k and k // (i + 1) > 1:
        singles.append((i, 1, k // (i + 1)))
        i += 1
    assert (k - i) % SUBLANES == 0
    groups = []
    for piece in sorted(singles, key=lambda p: -p[2]):
        for g in groups:
            if sum(p[2] for p in g) % SUBLANES and sum(p[2] for p in g) % SUBLANES + piece[2] <= SUBLANES:
                g.append(piece)
                break
        else:
            groups.append([piece])
    groups = [g + [None] * (-sum(p[2] for p in g) % SUBLANES) for g in groups]
    groups += [[(i0, SUBLANES, 1)] for i0 in range(i, k, SUBLANES)]
    return groups


PEERQ_GROUPS = _peerq_pairs(PEER_TOPK)


def _peerq_kernel(x_ref, nw_ref, wq_ref, keys_ref, h_ref, e_ref, g_ref, q_scr, e_scr, g_scr, *, tm):
    x = x_ref[...]
    ms = jnp.mean(x * x, axis=-1, keepdims=True)
    hb = ((x * lax.rsqrt(ms + EPS)) * nw_ref[...]).astype(BF16)
    h_ref[...] = hb.astype(F32)
    q_scr[...] = jnp.dot(hb, wq_ref[...], preferred_element_type=F32).astype(BF16)

    K = PEER_TOPK
    tl = PEERQ_LANES
    iota_k = lax.broadcasted_iota(I32, (PEER_NKEYS, tl), 0).astype(F32)
    pieces = [p for g in PEERQ_GROUPS for p in g]
    no_pair = float(K * K)

    def table(of_piece, pad):
        rows = [jnp.full((1, tl), pad, F32) if p is None else of_piece(*p) for p in pieces]
        return jnp.concatenate(rows, axis=0)

    def flat_of(i0, ni, nj):
        i = i0 + lax.broadcasted_iota(I32, (ni, tl), 0)
        j = lax.broadcasted_iota(I32, (nj, tl), 0)
        return (i * K + j).astype(F32)

    flat = table(flat_of, no_pair)

    def key_stage(h, toks):
        ex = []
        for p in range(2):
            off = pl.multiple_of(h * PEER_DKEY + p * PEER_DHALF, PEER_DHALF)
            qhp = q_scr[toks, pl.ds(off, PEER_DHALF)]
            s = lax.dot_general(keys_ref[p], qhp, (((1,), (1,)), ((), ())),
                                preferred_element_type=F32)
            ex.append(_Extraction(s, iota_k, float(PEER_NKEYS)))
        return ex

    def pair_stage(keys):
        sv = [e.values() for e in keys]
        si = [e.picks() for e in keys]
        cand = table(lambda i0, ni, nj: sv[0][i0:i0 + ni, :] + sv[1][0:nj, :], -jnp.inf)
        cidx = table(lambda i0, ni, nj: si[0][i0:i0 + ni, :] * PEER_NKEYS + si[1][0:nj, :], -1.0)
        return _Extraction(cand, flat, no_pair), cidx

    def heads(hb, carry):
        problems = [(hb * PEERQ_HEADS_PER_ITER + dh, slice(t0, t0 + tl))
                    for dh in range(PEERQ_HEADS_PER_ITER) for t0 in range(0, tm, tl)]
        pairs = None
        for n in range(len(problems) + 1):
            keys = key_stage(*problems[n]) if n < len(problems) else None
            eidx = []
            for _ in range(K):
                if keys is not None:
                    for e in keys:
                        e.step()
                if pairs is not None:
                    hit = pairs[0].step()
                    eidx.append(jnp.max(jnp.where(hit, pairs[1], -1.0), axis=0, keepdims=True))
            if pairs is not None:
                h, toks = problems[n - 1]
                best = pairs[0].values()
                ex = jnp.exp(best - best[0:1, :])
                row = pl.multiple_of(h * K, K)
                e_scr[pl.ds(row, K), toks] = jnp.concatenate(eidx, axis=0)
                g_scr[pl.ds(row, K), toks] = ex / jnp.sum(ex, axis=0, keepdims=True)
            pairs = pair_stage(keys) if keys is not None else None
        return carry

    lax.fori_loop(0, PEER_HEADS // PEERQ_HEADS_PER_ITER, heads, 0)
    e_ref[...] = e_scr[...].T.astype(I32)
    g_ref[...] = g_scr[...].T


def _peerq(x1, norm_w, wq_bf, keys_bf):
    T = x1.shape[0]
    tm = 256
    return pl.pallas_call(
        functools.partial(_peerq_kernel, tm=tm),
        grid=(T // tm,),
        in_specs=[
            pl.BlockSpec((tm, D_MODEL), lambda i: (i, 0)),
            pl.BlockSpec((1, D_MODEL), lambda i: (0, 0)),
            pl.BlockSpec((D_MODEL, PEER_HEADS * PEER_DKEY), lambda i: (0, 0)),
            pl.BlockSpec((2, PEER_NKEYS, PEER_DHALF), lambda i: (0, 0, 0)),
        ],
        out_specs=[
            pl.BlockSpec((tm, D_MODEL), lambda i: (i, 0)),
            pl.BlockSpec((tm, PEER_SEL), lambda i: (i, 0)),
            pl.BlockSpec((tm, PEER_SEL), lambda i: (i, 0)),
        ],
        out_shape=[
            jax.ShapeDtypeStruct((T, D_MODEL), F32),
            jax.ShapeDtypeStruct((T, PEER_SEL), I32),
            jax.ShapeDtypeStruct((T, PEER_SEL), F32),
        ],
        scratch_shapes=[pltpu.VMEM((tm, PEER_HEADS * PEER_DKEY), BF16),
                        pltpu.VMEM((PEER_SEL, tm), F32), pltpu.VMEM((PEER_SEL, tm), F32)],
        compiler_params=_cparams(1),
        name="peerq",
    )(x1, norm_w.reshape(1, D_MODEL), wq_bf, keys_bf)


PEER_SLOTS = 8
PEER_AHEAD = 6
PEER_TB = 128
PEER_NEXT = 8
PEER_NACC = 4


def _tree_sum(xs):
    while len(xs) > 1:
        xs = [a + b for a, b in zip(xs[0::2], xs[1::2])] + ([xs[-1]] if len(xs) % 2 else [])
    return xs[0]


def _peer_kernel(idx_ref, gate_ref, h_ref, hn_ref, tab_ref, y_ref, *scratch):
    bufs = scratch[:PEER_SLOTS]
    a_scr, sem = scratch[PEER_SLOTS:]

    def issue(tok, slot):
        for r in range(PEER_SEL):
            pltpu.make_async_copy(tab_ref.at[idx_ref[0, tok, r]], bufs[slot].at[r],
                                  sem.at[slot]).start(priority=r % 2)

    def wait(slot):
        pltpu.make_async_copy(tab_ref.at[pl.ds(0, PEER_SEL)], bufs[slot], sem.at[slot]).wait()

    S = SUBLANES
    lane = lax.broadcasted_iota(I32, (S, LANES), 1)
    sub = lax.broadcasted_iota(I32, (S, LANES), 0)

    def weights(tok, a_part):
        a_row = jnp.sum(a_part, axis=0, keepdims=True)
        act = 0.5 * a_row * (1.0 + lax.erf(a_row * np.float32(math.sqrt(0.5))))
        w_row = (act * gate_ref[pl.ds(tok, 1), :]).astype(BF16).astype(F32)
        return jnp.broadcast_to(w_row, (S, LANES))

    def rows(h, u_slot, v_tok, v_slot, a_prev):
        accs = [jnp.zeros((S, LANES), F32) for _ in range(PEER_NACC)]
        if v_tok is not None:
            w8 = weights(v_tok, a_prev)
            ys = [jnp.zeros((ROW_SUB, LANES), F32) for _ in range(PEER_NACC)]
        for r in range(PEER_SEL):
            c = r % PEER_NACC
            p = bufs[u_slot][r, 0:ROW_SUB, :].astype(F32) * h
            s = jnp.sum(p[0:S] + p[S:2 * S], axis=1, keepdims=True)
            accs[c] = jnp.where(lane == r, s, accs[c])
            if v_tok is not None:
                if r % S == 0:
                    w_col = jnp.sum(jnp.where(lane == sub + r, w8, 0.0), axis=1, keepdims=True)
                ys[c] = ys[c] + w_col[r % S:r % S + 1, :] * bufs[v_slot][r, ROW_SUB:2 * ROW_SUB, :].astype(F32)
        if v_tok is not None:
            y_ref[v_tok] = _tree_sum(ys)
        return _tree_sum(accs)

    step_i = pl.program_id(0)
    n_groups = PEER_TB // PEER_SLOTS

    @pl.when(step_i == 0)
    def _():
        for t in range(PEER_AHEAD + 1):
            issue(t, t)
        wait(0)
        a_scr[...] = rows(h_ref[0], 0, None, None, None)

    def group(g, a_prev):
        for k in range(PEER_SLOTS):
            j = g * PEER_SLOTS + k
            wait((k + 1) % PEER_SLOTS)
            issue(j + (PEER_AHEAD + 1), (k + PEER_AHEAD + 1) % PEER_SLOTS)
            if k + 1 < PEER_SLOTS:
                h = h_ref[j + 1]
            else:
                h = jnp.where(g == n_groups - 1, hn_ref[0], h_ref[jnp.minimum(j + 1, PEER_TB - 1)])
            a_prev = rows(h, (k + 1) % PEER_SLOTS, j, k, a_prev)
        return a_prev

    a_scr[...] = lax.fori_loop(0, n_groups, group, a_scr[...])

    @pl.when(step_i == pl.num_programs(0) - 1)
    def _():
        for t in range(PEER_TB + 1, PEER_TB + PEER_AHEAD + 1):
            wait(t % PEER_SLOTS)


def _peer(eidx, gate, h3, table):
    T = eidx.shape[0]
    tb = PEER_TB
    nb = T // tb
    idx3 = eidx.reshape(nb, tb, PEER_SEL)
    idx_ext = jnp.concatenate([idx3, jnp.roll(idx3[:, :PEER_NEXT], -1, axis=0)], axis=1)
    hn_blocks = T // PEER_NEXT
    return pl.pallas_call(
        _peer_kernel,
        grid=(nb,),
        in_specs=[
            pl.BlockSpec((1, tb + PEER_NEXT, PEER_SEL), lambda i: (i, 0, 0), memory_space=pltpu.SMEM),
            pl.BlockSpec((tb, PEER_SEL), lambda i: (i, 0)),
            pl.BlockSpec((tb, ROW_SUB, LANES), lambda i: (i, 0, 0)),
            pl.BlockSpec((PEER_NEXT, ROW_SUB, LANES),
                         lambda i: (jnp.minimum((i + 1) * (tb // PEER_NEXT), hn_blocks - 1), 0, 0)),
            pl.BlockSpec(memory_space=pl.ANY),
        ],
        out_specs=pl.BlockSpec((tb, ROW_SUB, LANES), lambda i: (i, 0, 0)),
        out_shape=jax.ShapeDtypeStruct((T, ROW_SUB, LANES), F32),
        scratch_shapes=(
            [pltpu.VMEM((PEER_SEL, 2 * ROW_SUB, LANES), BF16) for _ in range(PEER_SLOTS)]
            + [pltpu.VMEM((SUBLANES, LANES), F32), pltpu.SemaphoreType.DMA((PEER_SLOTS,))]),
        compiler_params=_cparams(1),
        name="peer",
    )(idx_ext, gate, h3, h3, table)


PACK_TE = 256


def _pack_kernel(u_ref, v_ref, o_ref):
    def group(g, carry):
        rows = pl.ds(pl.multiple_of(g * SUBLANES, SUBLANES), SUBLANES)
        for half, src in enumerate((u_ref, v_ref)):
            x = src[rows, :]
            chunks = jnp.stack([x[:, c * LANES:(c + 1) * LANES] for c in range(ROW_SUB)], axis=0)
            o_ref[rows, half * ROW_SUB:(half + 1) * ROW_SUB, :] = jnp.swapaxes(chunks, 0, 1).astype(BF16)
        return carry

    lax.fori_loop(0, PACK_TE // SUBLANES, group, 0)


def _pack_table(u, v):
    E = u.shape[0]
    return pl.pallas_call(
        _pack_kernel,
        grid=(E // PACK_TE,),
        in_specs=[pl.BlockSpec((PACK_TE, D_MODEL), lambda i: (i, 0)),
                  pl.BlockSpec((PACK_TE, D_MODEL), lambda i: (i, 0))],
        out_specs=pl.BlockSpec((PACK_TE, 2 * ROW_SUB, LANES), lambda i: (i, 0, 0)),
        out_shape=jax.ShapeDtypeStruct((E, 2 * ROW_SUB, LANES), BF16),
        compiler_params=_cparams(1),
        name="pack",
    )(u, v)


def _final_kernel(x_ref, y_ref, nw_ref, o_ref):
    x = x_ref[...] + y_ref[...]
    ms = jnp.mean(x * x, axis=-1, keepdims=True)
    o_ref[...] = (x * lax.rsqrt(ms + EPS)) * nw_ref[...]


def _final(x1, y, norm_w):
    T = x1.shape[0]
    tm = 512
    return pl.pallas_call(
        _final_kernel,
        grid=(T // tm,),
        in_specs=[
            pl.BlockSpec((tm, D_MODEL), lambda i: (i, 0)),
            pl.BlockSpec((tm, D_MODEL), lambda i: (i, 0)),
            pl.BlockSpec((1, D_MODEL), lambda i: (0, 0)),
        ],
        out_specs=pl.BlockSpec((tm, D_MODEL), lambda i: (i, 0)),
        out_shape=jax.ShapeDtypeStruct((T, D_MODEL), F32),
        compiler_params=_cparams(1),
        name="final",
    )(x1, y, norm_w.reshape(1, D_MODEL))


def kernel(x, norm1_w, w_in, w_ret_o, w_pool_lin, pool_scale, w_pool_o, w_out,
           norm2_w, peer_w_q, peer_sub_keys, peer_u, peer_v, final_norm_w):
    B, S, D = x.shape
    T = B * S
    assert D == D_MODEL and w_in.shape == (1, D_MODEL, IN_COLS), "one layer of the stated widths"
    x2 = x.reshape(T, D)
    proj = _inproj(x2, norm1_w[0], w_in[0].astype(BF16))
    proj3 = proj.reshape(B, S, IN_COLS)
    ret = _retention(proj3, _retention_constants(S))
    pooled = _pool(proj3, w_pool_lin[0].astype(BF16), pool_scale[0])
    mixed = _mix(ret.reshape(T, V_DIM), w_ret_o[0].astype(BF16),
                 pooled.reshape(T, POOL_DIM), w_pool_o[0].astype(BF16), proj)
    x1 = _resid(mixed, w_out[0].astype(BF16), x2)
    h2, eidx, gate = _peerq(x1, norm2_w[0], peer_w_q[0].astype(BF16),
                            peer_sub_keys[0].astype(BF16))
    table = _pack_table(peer_u[0], peer_v[0])
    y = _peer(eidx, gate, h2.reshape(T, ROW_SUB, LANES), table)
    out = _final(x1, y.reshape(T, D), final_norm_w)
    return out.reshape(B, S, D)
```

```python
import functools
import math

import numpy as np
import jax
import jax.numpy as jnp
from jax import lax
from jax.experimental import pallas as pl
from jax.experimental.pallas import tpu as pltpu

F32 = jnp.float32
BF16 = jnp.bfloat16
I32 = jnp.int32

D_MODEL = 2048
RET_HEADS = 8
RET_DK = 128
RET_DV = 256
RET_CHUNK = 128
POOL_WINDOWS = (2, 4, 8, 16)
POOL_GROUPS = len(POOL_WINDOWS)
POOL_DIM = 1024
POOL_GDIM = POOL_DIM // POOL_GROUPS
POOL_HALO = 16
PEER_HEADS = 8
PEER_NKEYS = 128
PEER_DKEY = 256
PEER_DHALF = PEER_DKEY // 2
PEER_TOPK = 16
PEER_SEL = PEER_HEADS * PEER_TOPK
ROPE_BASE = 10000.0
EPS = 1e-6

Q_DIM = RET_HEADS * RET_DK
V_DIM = RET_HEADS * RET_DV
OFF_Q = 0
OFF_K = OFF_Q + Q_DIM
OFF_V = OFF_K + Q_DIM
OFF_G = OFF_V + V_DIM
OFF_P = OFF_G + V_DIM
OFF_BG = OFF_P + POOL_DIM
IN_COLS = OFF_BG + 2 * D_MODEL

LANES = 128
SUBLANES = 8
ROW_SUB = D_MODEL // LANES
assert ROW_SUB == 2 * SUBLANES
VMEM_LIMIT = 48 * 1024 * 1024


def _cparams(n_axes):
    return pltpu.CompilerParams(
        dimension_semantics=("arbitrary",) * n_axes, vmem_limit_bytes=VMEM_LIMIT)


def _lanes_to_sublanes(x):
    chunks = jnp.stack([x[:, c * LANES:(c + 1) * LANES] for c in range(ROW_SUB)], axis=0)
    return jnp.swapaxes(chunks, 0, 1)


def _sublanes_to_lanes(t):
    chunks = jnp.swapaxes(t, 0, 1)
    return jnp.concatenate([chunks[c] for c in range(ROW_SUB)], axis=1)


def _inproj_kernel(x_ref, nw_ref, w_ref, o_ref, h_scr):
    @pl.when(pl.program_id(1) == 0)
    def _():
        x = x_ref[...]
        ms = jnp.mean(x * x, axis=-1, keepdims=True)
        h_scr[...] = ((x * lax.rsqrt(ms + EPS)) * nw_ref[...]).astype(BF16)

    o_ref[...] = jnp.dot(h_scr[...], w_ref[...], preferred_element_type=F32)


def _inproj(x2, norm_w, w_bf):
    T = x2.shape[0]
    tm, tn = 1024, 1024
    return pl.pallas_call(
        _inproj_kernel,
        grid=(T // tm, IN_COLS // tn),
        in_specs=[
            pl.BlockSpec((tm, D_MODEL), lambda i, j: (i, 0)),
            pl.BlockSpec((1, D_MODEL), lambda i, j: (0, 0)),
            pl.BlockSpec((D_MODEL, tn), lambda i, j: (0, j)),
        ],
        out_specs=pl.BlockSpec((tm, tn), lambda i, j: (i, j)),
        out_shape=jax.ShapeDtypeStruct((T, IN_COLS), F32),
        scratch_shapes=[pltpu.VMEM((tm, D_MODEL), BF16)],
        compiler_params=_cparams(2),
        name="inproj",
    )(x2, norm_w.reshape(1, D_MODEL), w_bf)


def _retention_constants(seq):
    H, C = RET_HEADS, RET_CHUNK
    half = RET_DK // 2
    inv_freq = ROPE_BASE ** (-jnp.arange(half, dtype=F32) / half)
    ang = jnp.arange(seq, dtype=F32)[:, None] * inv_freq[None, :]
    cos, sin = jnp.cos(ang), jnp.sin(ang)
    cos_full = jnp.concatenate([cos, cos], axis=-1)
    sin_signed = jnp.concatenate([-sin, sin], axis=-1)
    log_g = jnp.log(1.0 - 2.0 ** (-5.0 - jnp.arange(H, dtype=F32)))
    i = jnp.arange(C, dtype=F32)
    diff = i[:, None] - i[None, :]
    dmask = jnp.where(diff[None] >= 0,
                      jnp.exp(log_g[:, None, None] * jnp.maximum(diff, 0.0)[None]), 0.0)
    xi = jnp.exp(log_g[:, None] * (i[None, :] + 1.0))
    zeta = jnp.exp(log_g[:, None] * (C - 1.0 - i)[None, :])
    g_chunk = jnp.exp(log_g * C)
    xi_b = jnp.broadcast_to(xi[:, :, None], (H, C, RET_DK))
    zeta_b = jnp.broadcast_to(zeta[:, :, None], (H, C, RET_DK))
    g_b = jnp.broadcast_to(g_chunk[:, None, None], (H, SUBLANES, RET_DV))
    return cos_full, sin_signed, dmask, xi_b, zeta_b, g_b


def _ret_kernel(q_ref, k_ref, v_ref, g_ref, cos_ref, sin_ref, dm_ref, xi_ref, ze_ref, gc_ref,
                o_ref, r_scr, *, n_chunks):
    @pl.when(pl.program_id(1) == 0)
    def _():
        r_scr[...] = jnp.zeros_like(r_scr)

    scale = RET_DK ** -0.5
    for c in range(n_chunks):
        rows = slice(c * RET_CHUNK, (c + 1) * RET_CHUNK)
        cos = cos_ref[rows, :]
        sin = sin_ref[rows, :]
        for h in range(RET_HEADS):
            kc = slice(h * RET_DK, (h + 1) * RET_DK)
            vc = slice(h * RET_DV, (h + 1) * RET_DV)
            q = q_ref[0, rows, kc]
            k = k_ref[0, rows, kc]
            qr = q * cos + pltpu.roll(q, RET_DK // 2, 1) * sin
            kr = (k * cos + pltpu.roll(k, RET_DK // 2, 1) * sin) * scale
            vb = v_ref[0, rows, vc].astype(BF16)
            sc = lax.dot_general(qr.astype(BF16), kr.astype(BF16), (((1,), (1,)), ((), ())),
                                 preferred_element_type=F32) * dm_ref[h]
            inner = jnp.dot(sc.astype(BF16), vb, preferred_element_type=F32)
            state = r_scr[h]
            cross = jnp.dot((qr * xi_ref[h]).astype(BF16), state.astype(BF16),
                            preferred_element_type=F32)
            kzt = (kr * ze_ref[h]).T.astype(BF16)
            r_scr[h] = gc_ref[h, 0:1, :] * state + jnp.dot(kzt, vb, preferred_element_type=F32)
            o = inner + cross
            mu = jnp.mean(o, axis=-1, keepdims=True)
            var = jnp.mean((o - mu) ** 2, axis=-1, keepdims=True)
            y = (o - mu) * lax.rsqrt(var + EPS)
            g = g_ref[0, rows, vc]
            o_ref[0, rows, vc] = ((g * jax.nn.sigmoid(g)) * y).astype(BF16)


def _retention(proj3, consts):
    B, S, _ = proj3.shape
    n_chunks = 2
    tm = n_chunks * RET_CHUNK
    cos_full, sin_signed, dmask, xi_b, zeta_b, g_b = consts
    H, C = RET_HEADS, RET_CHUNK
    full3 = lambda b, j: (0, 0, 0)
    return pl.pallas_call(
        functools.partial(_ret_kernel, n_chunks=n_chunks),
        grid=(B, S // tm),
        in_specs=[
            pl.BlockSpec((1, tm, Q_DIM), lambda b, j: (b, j, OFF_Q // Q_DIM)),
            pl.BlockSpec((1, tm, Q_DIM), lambda b, j: (b, j, OFF_K // Q_DIM)),
            pl.BlockSpec((1, tm, V_DIM), lambda b, j: (b, j, OFF_V // V_DIM)),
            pl.BlockSpec((1, tm, V_DIM), lambda b, j: (b, j, OFF_G // V_DIM)),
            pl.BlockSpec((tm, RET_DK), lambda b, j: (j, 0)),
            pl.BlockSpec((tm, RET_DK), lambda b, j: (j, 0)),
            pl.BlockSpec((H, C, C), full3),
            pl.BlockSpec((H, C, RET_DK), full3),
            pl.BlockSpec((H, C, RET_DK), full3),
            pl.BlockSpec((H, SUBLANES, RET_DV), full3),
        ],
        out_specs=pl.BlockSpec((1, tm, V_DIM), lambda b, j: (b, j, 0)),
        out_shape=jax.ShapeDtypeStruct((B, S, V_DIM), BF16),
        scratch_shapes=[pltpu.VMEM((H, RET_DK, RET_DV), F32)],
        compiler_params=_cparams(2),
        name="retention",
    )(proj3, proj3, proj3, proj3, cos_full, sin_signed, dmask, xi_b, zeta_b, g_b)


def _pool_kernel(cur_ref, halo_ref, wl_ref, sc_ref, o_ref, *, tm):
    j = pl.program_id(1)
    cur = cur_ref[0]
    halo = jnp.where(j == 0, 0.0, halo_ref[0])
    ext = jnp.concatenate([halo, cur], axis=0)
    pos = j * tm + lax.broadcasted_iota(I32, (tm, 1), 0)
    for g, w in enumerate(POOL_WINDOWS):
        cols = slice(g * POOL_GDIM, (g + 1) * POOL_GDIM)
        s = ext[:, cols]
        sh = 1
        while sh < w:
            s = s + pltpu.roll(s, sh, 0)
            sh *= 2
        win = s[POOL_HALO:, :]
        count = jnp.minimum(pos + 1, w).astype(F32)
        pooled = win / count - cur[:, cols]
        lin = jnp.dot(pooled.astype(BF16), wl_ref[g], preferred_element_type=F32)
        o_ref[0, :, cols] = (lin * sc_ref[:, cols]).astype(BF16)


def _pool(proj3, wl_bf, pool_scale):
    B, S, _ = proj3.shape
    tm = 512
    return pl.pallas_call(
        functools.partial(_pool_kernel, tm=tm),
        grid=(B, S // tm),
        in_specs=[
            pl.BlockSpec((1, tm, POOL_DIM), lambda b, j: (b, j, OFF_P // POOL_DIM)),
            pl.BlockSpec((1, POOL_HALO, POOL_DIM),
                         lambda b, j: (b, jnp.maximum(j * (tm // POOL_HALO) - 1, 0), OFF_P // POOL_DIM)),
            pl.BlockSpec((POOL_GROUPS, POOL_GDIM, POOL_GDIM), lambda b, j: (0, 0, 0)),
            pl.BlockSpec((1, POOL_DIM), lambda b, j: (0, 0)),
        ],
        out_specs=pl.BlockSpec((1, tm, POOL_DIM), lambda b, j: (b, j, 0)),
        out_shape=jax.ShapeDtypeStruct((B, S, POOL_DIM), BF16),
        compiler_params=_cparams(2),
        name="pool",
    )(proj3, proj3, wl_bf, pool_scale.reshape(1, POOL_DIM))


def _mix_kernel(ret_ref, wr_ref, pool_ref, wp_ref, gr_ref, gp_ref, o_ref):
    y_ret = jnp.dot(ret_ref[...], wr_ref[...], preferred_element_type=F32)
    y_pool = jnp.dot(pool_ref[...], wp_ref[...], preferred_element_type=F32)
    mixed = jax.nn.sigmoid(gr_ref[...]) * y_ret + jax.nn.sigmoid(gp_ref[...]) * y_pool
    o_ref[...] = mixed.astype(BF16)


def _mix(ret2, wr_bf, pool2, wp_bf, proj2):
    T = ret2.shape[0]
    tm, tn = 1024, 512
    off_r = OFF_BG // tn
    off_p = (OFF_BG + D_MODEL) // tn
    return pl.pallas_call(
        _mix_kernel,
        grid=(T // tm, D_MODEL // tn),
        in_specs=[
            pl.BlockSpec((tm, V_DIM), lambda i, j: (i, 0)),
            pl.BlockSpec((V_DIM, tn), lambda i, j: (0, j)),
            pl.BlockSpec((tm, POOL_DIM), lambda i, j: (i, 0)),
            pl.BlockSpec((POOL_DIM, tn), lambda i, j: (0, j)),
            pl.BlockSpec((tm, tn), lambda i, j: (i, off_r + j)),
            pl.BlockSpec((tm, tn), lambda i, j: (i, off_p + j)),
        ],
        out_specs=pl.BlockSpec((tm, tn), lambda i, j: (i, j)),
        out_shape=jax.ShapeDtypeStruct((T, D_MODEL), BF16),
        compiler_params=_cparams(2),
        name="mix",
    )(ret2, wr_bf, pool2, wp_bf, proj2, proj2)


def _resid_kernel(m_ref, w_ref, x_ref, o_ref):
    o_ref[...] = x_ref[...] + jnp.dot(m_ref[...], w_ref[...], preferred_element_type=F32)


def _resid(mixed, w_bf, x2):
    T = x2.shape[0]
    tm, tn = 1024, 512
    return pl.pallas_call(
        _resid_kernel,
        grid=(T // tm, D_MODEL // tn),
        in_specs=[
            pl.BlockSpec((tm, D_MODEL), lambda i, j: (i, 0)),
            pl.BlockSpec((D_MODEL, tn), lambda i, j: (0, j)),
            pl.BlockSpec((tm, tn), lambda i, j: (i, j)),
        ],
        out_specs=pl.BlockSpec((tm, tn), lambda i, j: (i, j)),
        out_shape=jax.ShapeDtypeStruct((T, D_MODEL), F32),
        compiler_params=_cparams(2),
        name="resid",
    )(mixed, w_bf, x2)


class _Extraction:
    def __init__(self, s, ids, none_id):
        self.s, self.ids, self.none_id = s, ids, none_id
        self.vals, self.sel = [], []

    def step(self):
        m = jnp.max(self.s, axis=0, keepdims=True)
        pick = jnp.min(jnp.where(self.s == m, self.ids, self.none_id), axis=0, keepdims=True)
        hit = self.ids == pick
        self.vals.append(m)
        self.sel.append(pick)
        self.s = jnp.where(hit, -jnp.inf, self.s)
        return hit

    def values(self):
        return jnp.concatenate(self.vals, axis=0)

    def picks(self):
        return jnp.concatenate(self.sel, axis=0)


PEERQ_LANES = LANES
PEERQ_HEADS_PER_ITER = 4


def _peerq_pairs(k):
    singles, i = [], 0
    while i < k and k // (i + 1) > 1:
        singles.append((i, 1, k // (i + 1)))
        i += 1
    assert (k - i) % SUBLANES == 0
    groups = []
    for piece in sorted(singles, key=lambda p: -p[2]):
        for g in groups:
            if sum(p[2] for p in g) % SUBLANES and sum(p[2] for p in g) % SUBLANES + piece[2] <= SUBLANES:
                g.append(piece)
                break
        else:
            groups.append([piece])
    groups = [g + [None] * (-sum(p[2] for p in g) % SUBLANES) for g in groups]
    groups += [[(i0, SUBLANES, 1)] for i0 in range(i, k, SUBLANES)]
    return groups


PEERQ_GROUPS = _peerq_pairs(PEER_TOPK)


def _peerq_kernel(x_ref, nw_ref, wq_ref, keys_ref, h_ref, e_ref, g_ref, q_scr, e_scr, g_scr, h_scr,
                  *, tm):
    x = x_ref[...]
    ms = jnp.mean(x * x, axis=-1, keepdims=True)
    hb = ((x * lax.rsqrt(ms + EPS)) * nw_ref[...]).astype(BF16)
    q_scr[...] = jnp.dot(hb, wq_ref[...], preferred_element_type=F32).astype(BF16)
    h_scr[...] = hb.astype(F32)

    def h_tiles(g, carry):
        rows = pl.ds(pl.multiple_of(g * SUBLANES, SUBLANES), SUBLANES)
        h_ref[rows] = _lanes_to_sublanes(h_scr[rows, :])
        return carry

    lax.fori_loop(0, tm // SUBLANES, h_tiles, 0)

    K = PEER_TOPK
    tl = PEERQ_LANES
    iota_k = lax.broadcasted_iota(I32, (PEER_NKEYS, tl), 0).astype(F32)
    pieces = [p for g in PEERQ_GROUPS for p in g]
    no_pair = float(K * K)

    def table(of_piece, pad):
        rows = [jnp.full((1, tl), pad, F32) if p is None else of_piece(*p) for p in pieces]
        return jnp.concatenate(rows, axis=0)

    def flat_of(i0, ni, nj):
        i = i0 + lax.broadcasted_iota(I32, (ni, tl), 0)
        j = lax.broadcasted_iota(I32, (nj, tl), 0)
        return (i * K + j).astype(F32)

    flat = table(flat_of, no_pair)

    def key_stage(h, toks):
        ex = []
        for p in range(2):
            off = pl.multiple_of(h * PEER_DKEY + p * PEER_DHALF, PEER_DHALF)
            qhp = q_scr[toks, pl.ds(off, PEER_DHALF)]
            s = lax.dot_general(keys_ref[p], qhp, (((1,), (1,)), ((), ())),
                                preferred_element_type=F32)
            ex.append(_Extraction(s, iota_k, float(PEER_NKEYS)))
        return ex

    def pair_stage(keys):
        sv = [e.values() for e in keys]
        si = [e.picks() for e in keys]
        cand = table(lambda i0, ni, nj: sv[0][i0:i0 + ni, :] + sv[1][0:nj, :], -jnp.inf)
        cidx = table(lambda i0, ni, nj: si[0][i0:i0 + ni, :] * PEER_NKEYS + si[1][0:nj, :], -1.0)
        return _Extraction(cand, flat, no_pair), cidx

    def heads(hb, carry):
        problems = [(hb * PEERQ_HEADS_PER_ITER + dh, slice(t0, t0 + tl))
                    for dh in range(PEERQ_HEADS_PER_ITER) for t0 in range(0, tm, tl)]
        pairs = None
        for n in range(len(problems) + 1):
            keys = key_stage(*problems[n]) if n < len(problems) else None
            eidx = []
            for _ in range(K):
                if keys is not None:
                    for e in keys:
                        e.step()
                if pairs is not None:
                    hit = pairs[0].step()
                    eidx.append(jnp.max(jnp.where(hit, pairs[1], -1.0), axis=0, keepdims=True))
            if pairs is not None:
                h, toks = problems[n - 1]
                best = pairs[0].values()
                ex = jnp.exp(best - best[0:1, :])
                row = pl.multiple_of(h * K, K)
                e_scr[pl.ds(row, K), toks] = jnp.concatenate(eidx, axis=0)
                g_scr[pl.ds(row, K), toks] = ex / jnp.sum(ex, axis=0, keepdims=True)
            pairs = pair_stage(keys) if keys is not None else None
        return carry

    lax.fori_loop(0, PEER_HEADS // PEERQ_HEADS_PER_ITER, heads, 0)
    e_ref[...] = e_scr[...].T.astype(I32)
    g_ref[...] = g_scr[...].T


def _peerq(x1, norm_w, wq_bf, keys_bf):
    T = x1.shape[0]
    tm = 256
    return pl.pallas_call(
        functools.partial(_peerq_kernel, tm=tm),
        grid=(T // tm,),
        in_specs=[
            pl.BlockSpec((tm, D_MODEL), lambda i: (i, 0)),
            pl.BlockSpec((1, D_MODEL), lambda i: (0, 0)),
            pl.BlockSpec((D_MODEL, PEER_HEADS * PEER_DKEY), lambda i: (0, 0)),
            pl.BlockSpec((2, PEER_NKEYS, PEER_DHALF), lambda i: (0, 0, 0)),
        ],
        out_specs=[
            pl.BlockSpec((tm, ROW_SUB, LANES), lambda i: (i, 0, 0)),
            pl.BlockSpec((tm, PEER_SEL), lambda i: (i, 0)),
            pl.BlockSpec((tm, PEER_SEL), lambda i: (i, 0)),
        ],
        out_shape=[
            jax.ShapeDtypeStruct((T, ROW_SUB, LANES), F32),
            jax.ShapeDtypeStruct((T, PEER_SEL), I32),
            jax.ShapeDtypeStruct((T, PEER_SEL), F32),
        ],
        scratch_shapes=[pltpu.VMEM((tm, PEER_HEADS * PEER_DKEY), BF16),
                        pltpu.VMEM((PEER_SEL, tm), F32), pltpu.VMEM((PEER_SEL, tm), F32),
                        pltpu.VMEM((tm, D_MODEL), F32)],
        compiler_params=_cparams(1),
        name="peerq",
    )(x1, norm_w.reshape(1, D_MODEL), wq_bf, keys_bf)


PEER_SLOTS = 8
PEER_AHEAD = 6
PEER_TB = 128
PEER_NEXT = 8
PEER_NACC = 4


def _tree_sum(xs):
    while len(xs) > 1:
        xs = [a + b for a, b in zip(xs[0::2], xs[1::2])] + ([xs[-1]] if len(xs) % 2 else [])
    return xs[0]


def _peer_kernel(idx_ref, gate_ref, h_ref, hn_ref, tab_ref, y_ref, *scratch):
    bufs = scratch[:PEER_SLOTS]
    a_scr, sem = scratch[PEER_SLOTS:]

    def issue(tok, slot):
        for r in range(PEER_SEL):
            pltpu.make_async_copy(tab_ref.at[idx_ref[0, tok, r]], bufs[slot].at[r],
                                  sem.at[slot]).start(priority=r % 2)

    def wait(slot):
        pltpu.make_async_copy(tab_ref.at[pl.ds(0, PEER_SEL)], bufs[slot], sem.at[slot]).wait()

    S = SUBLANES
    lane = lax.broadcasted_iota(I32, (S, LANES), 1)
    sub = lax.broadcasted_iota(I32, (S, LANES), 0)

    def weights(tok, a_part):
        a_row = jnp.sum(a_part, axis=0, keepdims=True)
        act = 0.5 * a_row * (1.0 + lax.erf(a_row * np.float32(math.sqrt(0.5))))
        w_row = (act * gate_ref[pl.ds(tok, 1), :]).astype(BF16).astype(F32)
        return jnp.broadcast_to(w_row, (S, LANES))

    def rows(h, u_slot, v_tok, v_slot, a_prev):
        accs = [jnp.zeros((S, LANES), F32) for _ in range(PEER_NACC)]
        if v_tok is not None:
            w8 = weights(v_tok, a_prev)
            ys = [jnp.zeros((ROW_SUB, LANES), F32) for _ in range(PEER_NACC)]
        for r in range(PEER_SEL):
            c = r % PEER_NACC
            p = bufs[u_slot][r, 0:ROW_SUB, :].astype(F32) * h
            s = jnp.sum(p[0:S] + p[S:2 * S], axis=1, keepdims=True)
            accs[c] = jnp.where(lane == r, s, accs[c])
            if v_tok is not None:
                if r % S == 0:
                    w_col = jnp.sum(jnp.where(lane == sub + r, w8, 0.0), axis=1, keepdims=True)
                ys[c] = ys[c] + w_col[r % S:r % S + 1, :] * bufs[v_slot][r, ROW_SUB:2 * ROW_SUB, :].astype(F32)
        if v_tok is not None:
            y_ref[v_tok] = _tree_sum(ys)
        return _tree_sum(accs)

    step_i = pl.program_id(0)
    n_groups = PEER_TB // PEER_SLOTS

    @pl.when(step_i == 0)
    def _():
        for t in range(PEER_AHEAD + 1):
            issue(t, t)
        wait(0)
        a_scr[...] = rows(h_ref[0], 0, None, None, None)

    def group(g, a_prev):
        for k in range(PEER_SLOTS):
            j = g * PEER_SLOTS + k
            wait((k + 1) % PEER_SLOTS)
            issue(j + (PEER_AHEAD + 1), (k + PEER_AHEAD + 1) % PEER_SLOTS)
            if k + 1 < PEER_SLOTS:
                h = h_ref[j + 1]
            else:
                h = jnp.where(g == n_groups - 1, hn_ref[0], h_ref[jnp.minimum(j + 1, PEER_TB - 1)])
            a_prev = rows(h, (k + 1) % PEER_SLOTS, j, k, a_prev)
        return a_prev

    a_scr[...] = lax.fori_loop(0, n_groups, group, a_scr[...])

    @pl.when(step_i == pl.num_programs(0) - 1)
    def _():
        for t in range(PEER_TB + 1, PEER_TB + PEER_AHEAD + 1):
            wait(t % PEER_SLOTS)


def _peer(eidx, gate, h3, table):
    T = eidx.shape[0]
    tb = PEER_TB
    nb = T // tb
    idx3 = eidx.reshape(nb, tb, PEER_SEL)
    idx_ext = jnp.concatenate([idx3, jnp.roll(idx3[:, :PEER_NEXT], -1, axis=0)], axis=1)
    hn_blocks = T // PEER_NEXT
    return pl.pallas_call(
        _peer_kernel,
        grid=(nb,),
        in_specs=[
            pl.BlockSpec((1, tb + PEER_NEXT, PEER_SEL), lambda i: (i, 0, 0), memory_space=pltpu.SMEM),
            pl.BlockSpec((tb, PEER_SEL), lambda i: (i, 0)),
            pl.BlockSpec((tb, ROW_SUB, LANES), lambda i: (i, 0, 0)),
            pl.BlockSpec((PEER_NEXT, ROW_SUB, LANES),
                         lambda i: (jnp.minimum((i + 1) * (tb // PEER_NEXT), hn_blocks - 1), 0, 0)),
            pl.BlockSpec(memory_space=pl.ANY),
        ],
        out_specs=pl.BlockSpec((tb, ROW_SUB, LANES), lambda i: (i, 0, 0)),
        out_shape=jax.ShapeDtypeStruct((T, ROW_SUB, LANES), F32),
        scratch_shapes=(
            [pltpu.VMEM((PEER_SEL, 2 * ROW_SUB, LANES), BF16) for _ in range(PEER_SLOTS)]
            + [pltpu.VMEM((SUBLANES, LANES), F32), pltpu.SemaphoreType.DMA((PEER_SLOTS,))]),
        compiler_params=_cparams(1),
        name="peer",
    )(idx_ext, gate, h3, h3, table)


PACK_TE = 512


def _pack_kernel(u_ref, v_ref, o_ref):
    def group(g, carry):
        rows = pl.ds(pl.multiple_of(g * SUBLANES, SUBLANES), SUBLANES)
        for half, src in enumerate((u_ref, v_ref)):
            tiles = _lanes_to_sublanes(src[rows, :])
            o_ref[rows, half * ROW_SUB:(half + 1) * ROW_SUB, :] = tiles.astype(BF16)
        return carry

    lax.fori_loop(0, PACK_TE // SUBLANES, group, 0)


def _pack_table(u, v):
    E = u.shape[0]
    return pl.pallas_call(
        _pack_kernel,
        grid=(E // PACK_TE,),
        in_specs=[pl.BlockSpec((PACK_TE, D_MODEL), lambda i: (i, 0)),
                  pl.BlockSpec((PACK_TE, D_MODEL), lambda i: (i, 0))],
        out_specs=pl.BlockSpec((PACK_TE, 2 * ROW_SUB, LANES), lambda i: (i, 0, 0)),
        out_shape=jax.ShapeDtypeStruct((E, 2 * ROW_SUB, LANES), BF16),
        compiler_params=_cparams(1),
        name="pack",
    )(u, v)


def _final_kernel(x_ref, y_ref, nw_ref, o_ref, *, tm):
    unroll = 8

    def groups(gi, carry):
        for u in range(unroll):
            rows = pl.ds(pl.multiple_of((gi * unroll + u) * SUBLANES, SUBLANES), SUBLANES)
            x = x_ref[rows, :] + _sublanes_to_lanes(y_ref[rows])
            ms = jnp.mean(x * x, axis=-1, keepdims=True)
            o_ref[rows, :] = (x * lax.rsqrt(ms + EPS)) * nw_ref[...]
        return carry

    lax.fori_loop(0, tm // (SUBLANES * unroll), groups, 0)


def _final(x1, y, norm_w):
    T = x1.shape[0]
    tm = 512
    return pl.pallas_call(
        functools.partial(_final_kernel, tm=tm),
        grid=(T // tm,),
        in_specs=[
            pl.BlockSpec((tm, D_MODEL), lambda i: (i, 0)),
            pl.BlockSpec((tm, ROW_SUB, LANES), lambda i: (i, 0, 0)),
            pl.BlockSpec((1, D_MODEL), lambda i: (0, 0)),
        ],
        out_specs=pl.BlockSpec((tm, D_MODEL), lambda i: (i, 0)),
        out_shape=jax.ShapeDtypeStruct((T, D_MODEL), F32),
        compiler_params=_cparams(1),
        name="final",
    )(x1, y, norm_w.reshape(1, D_MODEL))


def kernel(x, norm1_w, w_in, w_ret_o, w_pool_lin, pool_scale, w_pool_o, w_out,
           norm2_w, peer_w_q, peer_sub_keys, peer_u, peer_v, final_norm_w):
    B, S, D = x.shape
    T = B * S
    assert D == D_MODEL and w_in.shape == (1, D_MODEL, IN_COLS), "one layer of the stated widths"
    x2 = x.reshape(T, D)
    proj = _inproj(x2, norm1_w[0], w_in[0].astype(BF16))
    proj3 = proj.reshape(B, S, IN_COLS)
    ret = _retention(proj3, _retention_constants(S))
    pooled = _pool(proj3, w_pool_lin[0].astype(BF16), pool_scale[0])
    mixed = _mix(ret.reshape(T, V_DIM), w_ret_o[0].astype(BF16),
                 pooled.reshape(T, POOL_DIM), w_pool_o[0].astype(BF16), proj)
    x1 = _resid(mixed, w_out[0].astype(BF16), x2)
    h2, eidx, gate = _peerq(x1, norm2_w[0], peer_w_q[0].astype(BF16),
                            peer_sub_keys[0].astype(BF16))
    table = _pack_table(peer_u[0], peer_v[0])
    y = _peer(eidx, gate, h2, table)
    out = _final(x1, y, final_norm_w)
    return out.reshape(B, S, D)
```

```python
import functools
import math

import numpy as np
import jax
import jax.numpy as jnp
from jax import lax
from jax.experimental import pallas as pl
from jax.experimental.pallas import tpu as pltpu

F32 = jnp.float32
BF16 = jnp.bfloat16
I32 = jnp.int32

D_MODEL = 2048
RET_HEADS = 8
RET_DK = 128
RET_DV = 256
RET_CHUNK = 128
POOL_WINDOWS = (2, 4, 8, 16)
POOL_GROUPS = len(POOL_WINDOWS)
POOL_DIM = 1024
POOL_GDIM = POOL_DIM // POOL_GROUPS
POOL_HALO = 16
PEER_HEADS = 8
PEER_NKEYS = 128
PEER_DKEY = 256
PEER_DHALF = PEER_DKEY // 2
PEER_TOPK = 16
PEER_SEL = PEER_HEADS * PEER_TOPK
ROPE_BASE = 10000.0
EPS = 1e-6

Q_DIM = RET_HEADS * RET_DK
V_DIM = RET_HEADS * RET_DV
OFF_Q = 0
OFF_K = OFF_Q + Q_DIM
OFF_V = OFF_K + Q_DIM
OFF_G = OFF_V + V_DIM
OFF_P = OFF_G + V_DIM
OFF_BG = OFF_P + POOL_DIM
IN_COLS = OFF_BG + 2 * D_MODEL

LANES = 128
SUBLANES = 8
ROW_SUB = D_MODEL // LANES
assert ROW_SUB == 2 * SUBLANES
VMEM_LIMIT = 48 * 1024 * 1024


def _cparams(n_axes):
    return pltpu.CompilerParams(
        dimension_semantics=("arbitrary",) * n_axes, vmem_limit_bytes=VMEM_LIMIT)


def _lanes_to_sublanes(x):
    chunks = jnp.stack([x[:, c * LANES:(c + 1) * LANES] for c in range(ROW_SUB)], axis=0)
    return jnp.swapaxes(chunks, 0, 1)


def _sublanes_to_lanes(t):
    chunks = jnp.swapaxes(t, 0, 1)
    return jnp.concatenate([chunks[c] for c in range(ROW_SUB)], axis=1)


def _inproj_kernel(x_ref, nw_ref, w_ref, o_ref, h_scr):
    @pl.when(pl.program_id(1) == 0)
    def _():
        x = x_ref[...]
        ms = jnp.mean(x * x, axis=-1, keepdims=True)
        h_scr[...] = ((x * lax.rsqrt(ms + EPS)) * nw_ref[...]).astype(BF16)

    o_ref[...] = jnp.dot(h_scr[...], w_ref[...], preferred_element_type=F32)


def _inproj(x2, norm_w, w_bf):
    T = x2.shape[0]
    tm, tn = 1024, 1024
    return pl.pallas_call(
        _inproj_kernel,
        grid=(T // tm, IN_COLS // tn),
        in_specs=[
            pl.BlockSpec((tm, D_MODEL), lambda i, j: (i, 0)),
            pl.BlockSpec((1, D_MODEL), lambda i, j: (0, 0)),
            pl.BlockSpec((D_MODEL, tn), lambda i, j: (0, j)),
        ],
        out_specs=pl.BlockSpec((tm, tn), lambda i, j: (i, j)),
        out_shape=jax.ShapeDtypeStruct((T, IN_COLS), F32),
        scratch_shapes=[pltpu.VMEM((tm, D_MODEL), BF16)],
        compiler_params=_cparams(2),
        name="inproj",
    )(x2, norm_w.reshape(1, D_MODEL), w_bf)


def _retention_constants(seq):
    H, C = RET_HEADS, RET_CHUNK
    half = RET_DK // 2
    inv_freq = ROPE_BASE ** (-jnp.arange(half, dtype=F32) / half)
    ang = jnp.arange(seq, dtype=F32)[:, None] * inv_freq[None, :]
    cos, sin = jnp.cos(ang), jnp.sin(ang)
    cos_full = jnp.concatenate([cos, cos], axis=-1)
    sin_signed = jnp.concatenate([-sin, sin], axis=-1)
    log_g = jnp.log(1.0 - 2.0 ** (-5.0 - jnp.arange(H, dtype=F32)))
    i = jnp.arange(C, dtype=F32)
    diff = i[:, None] - i[None, :]
    dmask = jnp.where(diff[None] >= 0,
                      jnp.exp(log_g[:, None, None] * jnp.maximum(diff, 0.0)[None]), 0.0)
    xi = jnp.exp(log_g[:, None] * (i[None, :] + 1.0))
    zeta = jnp.exp(log_g[:, None] * (C - 1.0 - i)[None, :])
    g_chunk = jnp.exp(log_g * C)
    xi_b = jnp.broadcast_to(xi[:, :, None], (H, C, RET_DK))
    zeta_b = jnp.broadcast_to(zeta[:, :, None], (H, C, RET_DK))
    g_b = jnp.broadcast_to(g_chunk[:, None, None], (H, SUBLANES, RET_DV))
    return cos_full, sin_signed, dmask, xi_b, zeta_b, g_b


def _ret_kernel(q_ref, k_ref, v_ref, g_ref, cos_ref, sin_ref, dm_ref, xi_ref, ze_ref, gc_ref,
                o_ref, r_scr, *, n_chunks):
    @pl.when(pl.program_id(1) == 0)
    def _():
        r_scr[...] = jnp.zeros_like(r_scr)

    scale = RET_DK ** -0.5
    for c in range(n_chunks):
        rows = slice(c * RET_CHUNK, (c + 1) * RET_CHUNK)
        cos = cos_ref[rows, :]
        sin = sin_ref[rows, :]
        for h in range(RET_HEADS):
            kc = slice(h * RET_DK, (h + 1) * RET_DK)
            vc = slice(h * RET_DV, (h + 1) * RET_DV)
            q = q_ref[0, rows, kc]
            k = k_ref[0, rows, kc]
            qr = q * cos + pltpu.roll(q, RET_DK // 2, 1) * sin
            kr = (k * cos + pltpu.roll(k, RET_DK // 2, 1) * sin) * scale
            vb = v_ref[0, rows, vc].astype(BF16)
            sc = lax.dot_general(qr.astype(BF16), kr.astype(BF16), (((1,), (1,)), ((), ())),
                                 preferred_element_type=F32) * dm_ref[h]
            inner = jnp.dot(sc.astype(BF16), vb, preferred_element_type=F32)
            state = r_scr[h]
            cross = jnp.dot((qr * xi_ref[h]).astype(BF16), state.astype(BF16),
                            preferred_element_type=F32)
            kzt = (kr * ze_ref[h]).T.astype(BF16)
            r_scr[h] = gc_ref[h, 0:1, :] * state + jnp.dot(kzt, vb, preferred_element_type=F32)
            o = inner + cross
            mu = jnp.mean(o, axis=-1, keepdims=True)
            var = jnp.mean((o - mu) ** 2, axis=-1, keepdims=True)
            y = (o - mu) * lax.rsqrt(var + EPS)
            g = g_ref[0, rows, vc]
            o_ref[0, rows, vc] = ((g * jax.nn.sigmoid(g)) * y).astype(BF16)


def _retention(proj3, consts):
    B, S, _ = proj3.shape
    n_chunks = 2
    tm = n_chunks * RET_CHUNK
    cos_full, sin_signed, dmask, xi_b, zeta_b, g_b = consts
    H, C = RET_HEADS, RET_CHUNK
    full3 = lambda b, j: (0, 0, 0)
    return pl.pallas_call(
        functools.partial(_ret_kernel, n_chunks=n_chunks),
        grid=(B, S // tm),
        in_specs=[
            pl.BlockSpec((1, tm, Q_DIM), lambda b, j: (b, j, OFF_Q // Q_DIM)),
            pl.BlockSpec((1, tm, Q_DIM), lambda b, j: (b, j, OFF_K // Q_DIM)),
            pl.BlockSpec((1, tm, V_DIM), lambda b, j: (b, j, OFF_V // V_DIM)),
            pl.BlockSpec((1, tm, V_DIM), lambda b, j: (b, j, OFF_G // V_DIM)),
            pl.BlockSpec((tm, RET_DK), lambda b, j: (j, 0)),
            pl.BlockSpec((tm, RET_DK), lambda b, j: (j, 0)),
            pl.BlockSpec((H, C, C), full3),
            pl.BlockSpec((H, C, RET_DK), full3),
            pl.BlockSpec((H, C, RET_DK), full3),
            pl.BlockSpec((H, SUBLANES, RET_DV), full3),
        ],
        out_specs=pl.BlockSpec((1, tm, V_DIM), lambda b, j: (b, j, 0)),
        out_shape=jax.ShapeDtypeStruct((B, S, V_DIM), BF16),
        scratch_shapes=[pltpu.VMEM((H, RET_DK, RET_DV), F32)],
        compiler_params=_cparams(2),
        name="retention",
    )(proj3, proj3, proj3, proj3, cos_full, sin_signed, dmask, xi_b, zeta_b, g_b)


def _pool_kernel(cur_ref, halo_ref, wl_ref, sc_ref, o_ref, *, tm):
    j = pl.program_id(1)
    cur = cur_ref[0]
    halo = jnp.where(j == 0, 0.0, halo_ref[0])
    ext = jnp.concatenate([halo, cur], axis=0)
    pos = j * tm + lax.broadcasted_iota(I32, (tm, 1), 0)
    for g, w in enumerate(POOL_WINDOWS):
        cols = slice(g * POOL_GDIM, (g + 1) * POOL_GDIM)
        s = ext[:, cols]
        sh = 1
        while sh < w:
            s = s + pltpu.roll(s, sh, 0)
            sh *= 2
        win = s[POOL_HALO:, :]
        count = jnp.minimum(pos + 1, w).astype(F32)
        pooled = win / count - cur[:, cols]
        lin = jnp.dot(pooled.astype(BF16), wl_ref[g], preferred_element_type=F32)
        o_ref[0, :, cols] = (lin * sc_ref[:, cols]).astype(BF16)


def _pool(proj3, wl_bf, pool_scale):
    B, S, _ = proj3.shape
    tm = 512
    return pl.pallas_call(
        functools.partial(_pool_kernel, tm=tm),
        grid=(B, S // tm),
        in_specs=[
            pl.BlockSpec((1, tm, POOL_DIM), lambda b, j: (b, j, OFF_P // POOL_DIM)),
            pl.BlockSpec((1, POOL_HALO, POOL_DIM),
                         lambda b, j: (b, jnp.maximum(j * (tm // POOL_HALO) - 1, 0), OFF_P // POOL_DIM)),
            pl.BlockSpec((POOL_GROUPS, POOL_GDIM, POOL_GDIM), lambda b, j: (0, 0, 0)),
            pl.BlockSpec((1, POOL_DIM), lambda b, j: (0, 0)),
        ],
        out_specs=pl.BlockSpec((1, tm, POOL_DIM), lambda b, j: (b, j, 0)),
        out_shape=jax.ShapeDtypeStruct((B, S, POOL_DIM), BF16),
        compiler_params=_cparams(2),
        name="pool",
    )(proj3, proj3, wl_bf, pool_scale.reshape(1, POOL_DIM))


def _mix_kernel(ret_ref, wr_ref, pool_ref, wp_ref, gr_ref, gp_ref, o_ref):
    y_ret = jnp.dot(ret_ref[...], wr_ref[...], preferred_element_type=F32)
    y_pool = jnp.dot(pool_ref[...], wp_ref[...], preferred_element_type=F32)
    mixed = jax.nn.sigmoid(gr_ref[...]) * y_ret + jax.nn.sigmoid(gp_ref[...]) * y_pool
    o_ref[...] = mixed.astype(BF16)


def _mix(ret2, wr_bf, pool2, wp_bf, proj2):
    T = ret2.shape[0]
    tm, tn = 512, 1024
    off_r = OFF_BG // tn
    off_p = (OFF_BG + D_MODEL) // tn
    return pl.pallas_call(
        _mix_kernel,
        grid=(T // tm, D_MODEL // tn),
        in_specs=[
            pl.BlockSpec((tm, V_DIM), lambda i, j: (i, 0)),
            pl.BlockSpec((V_DIM, tn), lambda i, j: (0, j)),
            pl.BlockSpec((tm, POOL_DIM), lambda i, j: (i, 0)),
            pl.BlockSpec((POOL_DIM, tn), lambda i, j: (0, j)),
            pl.BlockSpec((tm, tn), lambda i, j: (i, off_r + j)),
            pl.BlockSpec((tm, tn), lambda i, j: (i, off_p + j)),
        ],
        out_specs=pl.BlockSpec((tm, tn), lambda i, j: (i, j)),
        out_shape=jax.ShapeDtypeStruct((T, D_MODEL), BF16),
        compiler_params=_cparams(2),
        name="mix",
    )(ret2, wr_bf, pool2, wp_bf, proj2, proj2)


def _resid_kernel(m_ref, w_ref, x_ref, o_ref):
    o_ref[...] = x_ref[...] + jnp.dot(m_ref[...], w_ref[...], preferred_element_type=F32)


def _resid(mixed, w_bf, x2):
    T = x2.shape[0]
    tm, tn = 1024, 1024
    return pl.pallas_call(
        _resid_kernel,
        grid=(T // tm, D_MODEL // tn),
        in_specs=[
            pl.BlockSpec((tm, D_MODEL), lambda i, j: (i, 0)),
            pl.BlockSpec((D_MODEL, tn), lambda i, j: (0, j)),
            pl.BlockSpec((tm, tn), lambda i, j: (i, j)),
        ],
        out_specs=pl.BlockSpec((tm, tn), lambda i, j: (i, j)),
        out_shape=jax.ShapeDtypeStruct((T, D_MODEL), F32),
        compiler_params=_cparams(2),
        name="resid",
    )(mixed, w_bf, x2)


class _Extraction:
    def __init__(self, s, ids, none_id):
        self.s, self.ids, self.none_id = s, ids, none_id
        self.vals, self.sel = [], []

    def step(self):
        m = jnp.max(self.s, axis=0, keepdims=True)
        pick = jnp.min(jnp.where(self.s == m, self.ids, self.none_id), axis=0, keepdims=True)
        hit = self.ids == pick
        self.vals.append(m)
        self.sel.append(pick)
        self.s = jnp.where(hit, -jnp.inf, self.s)
        return hit

    def values(self):
        return jnp.concatenate(self.vals, axis=0)

    def picks(self):
        return jnp.concatenate(self.sel, axis=0)


PEERQ_LANES = LANES
PEERQ_HEADS_PER_ITER = 4


def _peerq_pairs(k):
    singles, i = [], 0
    while i < k and k // (i + 1) > 1:
        singles.append((i, 1, k // (i + 1)))
        i += 1
    assert (k - i) % SUBLANES == 0
    groups = []
    for piece in sorted(singles, key=lambda p: -p[2]):
        for g in groups:
            if sum(p[2] for p in g) % SUBLANES and sum(p[2] for p in g) % SUBLANES + piece[2] <= SUBLANES:
                g.append(piece)
                break
        else:
            groups.append([piece])
    groups = [g + [None] * (-sum(p[2] for p in g) % SUBLANES) for g in groups]
    groups += [[(i0, SUBLANES, 1)] for i0 in range(i, k, SUBLANES)]
    return groups


PEERQ_GROUPS = _peerq_pairs(PEER_TOPK)


def _peerq_kernel(x_ref, nw_ref, wq_ref, keys_ref, h_ref, e_ref, g_ref, q_scr, e_scr, g_scr, h_scr,
                  *, tm):
    x = x_ref[...]
    ms = jnp.mean(x * x, axis=-1, keepdims=True)
    hb = ((x * lax.rsqrt(ms + EPS)) * nw_ref[...]).astype(BF16)
    q_scr[...] = jnp.dot(hb, wq_ref[...], preferred_element_type=F32).astype(BF16)
    h_scr[...] = hb.astype(F32)

    def h_tiles(g, carry):
        rows = pl.ds(pl.multiple_of(g * SUBLANES, SUBLANES), SUBLANES)
        h_ref[rows] = _lanes_to_sublanes(h_scr[rows, :])
        return carry

    lax.fori_loop(0, tm // SUBLANES, h_tiles, 0)

    K = PEER_TOPK
    tl = PEERQ_LANES
    iota_k = lax.broadcasted_iota(I32, (PEER_NKEYS, tl), 0).astype(F32)
    pieces = [p for g in PEERQ_GROUPS for p in g]
    no_pair = float(K * K)

    def table(of_piece, pad):
        rows = [jnp.full((1, tl), pad, F32) if p is None else of_piece(*p) for p in pieces]
        return jnp.concatenate(rows, axis=0)

    def flat_of(i0, ni, nj):
        i = i0 + lax.broadcasted_iota(I32, (ni, tl), 0)
        j = lax.broadcasted_iota(I32, (nj, tl), 0)
        return (i * K + j).astype(F32)

    flat = table(flat_of, no_pair)

    def key_stage(h, toks):
        ex = []
        for p in range(2):
            off = pl.multiple_of(h * PEER_DKEY + p * PEER_DHALF, PEER_DHALF)
            qhp = q_scr[toks, pl.ds(off, PEER_DHALF)]
            s = lax.dot_general(keys_ref[p], qhp, (((1,), (1,)), ((), ())),
                                preferred_element_type=F32)
            ex.append(_Extraction(s, iota_k, float(PEER_NKEYS)))
        return ex

    def pair_stage(keys):
        sv = [e.values() for e in keys]
        si = [e.picks() for e in keys]
        cand = table(lambda i0, ni, nj: sv[0][i0:i0 + ni, :] + sv[1][0:nj, :], -jnp.inf)
        cidx = table(lambda i0, ni, nj: si[0][i0:i0 + ni, :] * PEER_NKEYS + si[1][0:nj, :], -1.0)
        return _Extraction(cand, flat, no_pair), cidx

    def heads(hb, carry):
        problems = [(hb * PEERQ_HEADS_PER_ITER + dh, slice(t0, t0 + tl))
                    for dh in range(PEERQ_HEADS_PER_ITER) for t0 in range(0, tm, tl)]
        pairs = None
        for n in range(len(problems) + 1):
            keys = key_stage(*problems[n]) if n < len(problems) else None
            eidx = []
            for _ in range(K):
                if keys is not None:
                    for e in keys:
                        e.step()
                if pairs is not None:
                    hit = pairs[0].step()
                    eidx.append(jnp.max(jnp.where(hit, pairs[1], -1.0), axis=0, keepdims=True))
            if pairs is not None:
                h, toks = problems[n - 1]
                best = pairs[0].values()
                ex = jnp.exp(best - best[0:1, :])
                row = pl.multiple_of(h * K, K)
                e_scr[pl.ds(row, K), toks] = jnp.concatenate(eidx, axis=0)
                g_scr[pl.ds(row, K), toks] = ex / jnp.sum(ex, axis=0, keepdims=True)
            pairs = pair_stage(keys) if keys is not None else None
        return carry

    lax.fori_loop(0, PEER_HEADS // PEERQ_HEADS_PER_ITER, heads, 0)
    e_ref[...] = e_scr[...].T.astype(I32)
    g_ref[...] = g_scr[...].T


def _peerq(x1, norm_w, wq_bf, keys_bf):
    T = x1.shape[0]
    tm = 256
    return pl.pallas_call(
        functools.partial(_peerq_kernel, tm=tm),
        grid=(T // tm,),
        in_specs=[
            pl.BlockSpec((tm, D_MODEL), lambda i: (i, 0)),
            pl.BlockSpec((1, D_MODEL), lambda i: (0, 0)),
            pl.BlockSpec((D_MODEL, PEER_HEADS * PEER_DKEY), lambda i: (0, 0)),
            pl.BlockSpec((2, PEER_NKEYS, PEER_DHALF), lambda i: (0, 0, 0)),
        ],
        out_specs=[
            pl.BlockSpec((tm, ROW_SUB, LANES), lambda i: (i, 0, 0)),
            pl.BlockSpec((tm, PEER_SEL), lambda i: (i, 0)),
            pl.BlockSpec((tm, PEER_SEL), lambda i: (i, 0)),
        ],
        out_shape=[
            jax.ShapeDtypeStruct((T, ROW_SUB, LANES), F32),
            jax.ShapeDtypeStruct((T, PEER_SEL), I32),
            jax.ShapeDtypeStruct((T, PEER_SEL), F32),
        ],
        scratch_shapes=[pltpu.VMEM((tm, PEER_HEADS * PEER_DKEY), BF16),
                        pltpu.VMEM((PEER_SEL, tm), F32), pltpu.VMEM((PEER_SEL, tm), F32),
                        pltpu.VMEM((tm, D_MODEL), F32)],
        compiler_params=_cparams(1),
        name="peerq",
    )(x1, norm_w.reshape(1, D_MODEL), wq_bf, keys_bf)


PEER_SLOTS = 8
PEER_AHEAD = 6
PEER_TB = 128
PEER_NEXT = 8
PEER_NACC = 4


def _tree_sum(xs):
    while len(xs) > 1:
        xs = [a + b for a, b in zip(xs[0::2], xs[1::2])] + ([xs[-1]] if len(xs) % 2 else [])
    return xs[0]


def _peer_kernel(idx_ref, gate_ref, h_ref, hn_ref, tab_ref, y_ref, *scratch):
    bufs = scratch[:PEER_SLOTS]
    a_scr, sem = scratch[PEER_SLOTS:]

    def issue(tok, slot):
        for r in range(PEER_SEL):
            pltpu.make_async_copy(tab_ref.at[idx_ref[0, tok, r]], bufs[slot].at[r],
                                  sem.at[slot]).start(priority=r % 2)

    def wait(slot):
        pltpu.make_async_copy(tab_ref.at[pl.ds(0, PEER_SEL)], bufs[slot], sem.at[slot]).wait()

    S = SUBLANES
    lane = lax.broadcasted_iota(I32, (S, LANES), 1)
    sub = lax.broadcasted_iota(I32, (S, LANES), 0)

    def weights(tok, a_part):
        a_row = jnp.sum(a_part, axis=0, keepdims=True)
        act = 0.5 * a_row * (1.0 + lax.erf(a_row * np.float32(math.sqrt(0.5))))
        w_row = (act * gate_ref[pl.ds(tok, 1), :]).astype(BF16).astype(F32)
        return jnp.broadcast_to(w_row, (S, LANES))

    def rows(h, u_slot, v_tok, v_slot, a_prev):
        accs = [jnp.zeros((S, LANES), F32) for _ in range(PEER_NACC)]
        if v_tok is not None:
            w8 = weights(v_tok, a_prev)
            ys = [jnp.zeros((ROW_SUB, LANES), F32) for _ in range(PEER_NACC)]
        for r in range(PEER_SEL):
            c = r % PEER_NACC
            p = bufs[u_slot][r, 0:ROW_SUB, :].astype(F32) * h
            s = jnp.sum(p[0:S] + p[S:2 * S], axis=1, keepdims=True)
            accs[c] = jnp.where(lane == r, s, accs[c])
            if v_tok is not None:
                if r % S == 0:
                    w_col = jnp.sum(jnp.where(lane == sub + r, w8, 0.0), axis=1, keepdims=True)
                ys[c] = ys[c] + w_col[r % S:r % S + 1, :] * bufs[v_slot][r, ROW_SUB:2 * ROW_SUB, :].astype(F32)
        if v_tok is not None:
            y_ref[v_tok] = _tree_sum(ys)
        return _tree_sum(accs)

    step_i = pl.program_id(0)
    n_groups = PEER_TB // PEER_SLOTS

    @pl.when(step_i == 0)
    def _():
        for t in range(PEER_AHEAD + 1):
            issue(t, t)
        wait(0)
        a_scr[...] = rows(h_ref[0], 0, None, None, None)

    def group(g, a_prev):
        for k in range(PEER_SLOTS):
            j = g * PEER_SLOTS + k
            wait((k + 1) % PEER_SLOTS)
            issue(j + (PEER_AHEAD + 1), (k + PEER_AHEAD + 1) % PEER_SLOTS)
            if k + 1 < PEER_SLOTS:
                h = h_ref[j + 1]
            else:
                h = jnp.where(g == n_groups - 1, hn_ref[0], h_ref[jnp.minimum(j + 1, PEER_TB - 1)])
            a_prev = rows(h, (k + 1) % PEER_SLOTS, j, k, a_prev)
        return a_prev

    a_scr[...] = lax.fori_loop(0, n_groups, group, a_scr[...])

    @pl.when(step_i == pl.num_programs(0) - 1)
    def _():
        for t in range(PEER_TB + 1, PEER_TB + PEER_AHEAD + 1):
            wait(t % PEER_SLOTS)


def _peer(eidx, gate, h3, table):
    T = eidx.shape[0]
    tb = PEER_TB
    nb = T // tb
    idx3 = eidx.reshape(nb, tb, PEER_SEL)
    idx_ext = jnp.concatenate([idx3, jnp.roll(idx3[:, :PEER_NEXT], -1, axis=0)], axis=1)
    hn_blocks = T // PEER_NEXT
    return pl.pallas_call(
        _peer_kernel,
        grid=(nb,),
        in_specs=[
            pl.BlockSpec((1, tb + PEER_NEXT, PEER_SEL), lambda i: (i, 0, 0), memory_space=pltpu.SMEM),
            pl.BlockSpec((tb, PEER_SEL), lambda i: (i, 0)),
            pl.BlockSpec((tb, ROW_SUB, LANES), lambda i: (i, 0, 0)),
            pl.BlockSpec((PEER_NEXT, ROW_SUB, LANES),
                         lambda i: (jnp.minimum((i + 1) * (tb // PEER_NEXT), hn_blocks - 1), 0, 0)),
            pl.BlockSpec(memory_space=pl.ANY),
        ],
        out_specs=pl.BlockSpec((tb, ROW_SUB, LANES), lambda i: (i, 0, 0)),
        out_shape=jax.ShapeDtypeStruct((T, ROW_SUB, LANES), F32),
        scratch_shapes=(
            [pltpu.VMEM((PEER_SEL, 2 * ROW_SUB, LANES), BF16) for _ in range(PEER_SLOTS)]
            + [pltpu.VMEM((SUBLANES, LANES), F32), pltpu.SemaphoreType.DMA((PEER_SLOTS,))]),
        compiler_params=_cparams(1),
        name="peer",
    )(idx_ext, gate, h3, h3, table)


PACK_TE = 512


def _pack_kernel(u_ref, v_ref, o_ref):
    def group(g, carry):
        rows = pl.ds(pl.multiple_of(g * SUBLANES, SUBLANES), SUBLANES)
        for half, src in enumerate((u_ref, v_ref)):
            tiles = _lanes_to_sublanes(src[rows, :])
            o_ref[rows, half * ROW_SUB:(half + 1) * ROW_SUB, :] = tiles.astype(BF16)
        return carry

    lax.fori_loop(0, PACK_TE // SUBLANES, group, 0)


def _pack_table(u, v):
    E = u.shape[0]
    return pl.pallas_call(
        _pack_kernel,
        grid=(E // PACK_TE,),
        in_specs=[pl.BlockSpec((PACK_TE, D_MODEL), lambda i: (i, 0)),
                  pl.BlockSpec((PACK_TE, D_MODEL), lambda i: (i, 0))],
        out_specs=pl.BlockSpec((PACK_TE, 2 * ROW_SUB, LANES), lambda i: (i, 0, 0)),
        out_shape=jax.ShapeDtypeStruct((E, 2 * ROW_SUB, LANES), BF16),
        compiler_params=_cparams(1),
        name="pack",
    )(u, v)


def _final_kernel(x_ref, y_ref, nw_ref, o_ref, *, tm):
    unroll = 8

    def groups(gi, carry):
        for u in range(unroll):
            rows = pl.ds(pl.multiple_of((gi * unroll + u) * SUBLANES, SUBLANES), SUBLANES)
            x = x_ref[rows, :] + _sublanes_to_lanes(y_ref[rows])
            ms = jnp.mean(x * x, axis=-1, keepdims=True)
            o_ref[rows, :] = (x * lax.rsqrt(ms + EPS)) * nw_ref[...]
        return carry

    lax.fori_loop(0, tm // (SUBLANES * unroll), groups, 0)


def _final(x1, y, norm_w):
    T = x1.shape[0]
    tm = 512
    return pl.pallas_call(
        functools.partial(_final_kernel, tm=tm),
        grid=(T // tm,),
        in_specs=[
            pl.BlockSpec((tm, D_MODEL), lambda i: (i, 0)),
            pl.BlockSpec((tm, ROW_SUB, LANES), lambda i: (i, 0, 0)),
            pl.BlockSpec((1, D_MODEL), lambda i: (0, 0)),
        ],
        out_specs=pl.BlockSpec((tm, D_MODEL), lambda i: (i, 0)),
        out_shape=jax.ShapeDtypeStruct((T, D_MODEL), F32),
        compiler_params=_cparams(1),
        name="final",
    )(x1, y, norm_w.reshape(1, D_MODEL))


def kernel(x, norm1_w, w_in, w_ret_o, w_pool_lin, pool_scale, w_pool_o, w_out,
           norm2_w, peer_w_q, peer_sub_keys, peer_u, peer_v, final_norm_w):
    B, S, D = x.shape
    T = B * S
    assert D == D_MODEL and w_in.shape == (1, D_MODEL, IN_COLS), "one layer of the stated widths"
    x2 = x.reshape(T, D)
    proj = _inproj(x2, norm1_w[0], w_in[0].astype(BF16))
    proj3 = proj.reshape(B, S, IN_COLS)
    ret = _retention(proj3, _retention_constants(S))
    pooled = _pool(proj3, w_pool_lin[0].astype(BF16), pool_scale[0])
    mixed = _mix(ret.reshape(T, V_DIM), w_ret_o[0].astype(BF16),
                 pooled.reshape(T, POOL_DIM), w_pool_o[0].astype(BF16), proj)
    x1 = _resid(mixed, w_out[0].astype(BF16), x2)
    h2, eidx, gate = _peerq(x1, norm2_w[0], peer_w_q[0].astype(BF16),
                            peer_sub_keys[0].astype(BF16))
    table = _pack_table(peer_u[0], peer_v[0])
    y = _peer(eidx, gate, h2, table)
    out = _final(x1, y, final_norm_w)
    return out.reshape(B, S, D)
```

```python
import functools
import math

import numpy as np
import jax
import jax.numpy as jnp
from jax import lax
from jax.experimental import pallas as pl
from jax.experimental.pallas import tpu as pltpu

F32 = jnp.float32
BF16 = jnp.bfloat16
I32 = jnp.int32

D_MODEL = 2048
RET_HEADS = 8
RET_DK = 128
RET_DV = 256
RET_CHUNK = 128
POOL_WINDOWS = (2, 4, 8, 16)
POOL_GROUPS = len(POOL_WINDOWS)
POOL_DIM = 1024
POOL_GDIM = POOL_DIM // POOL_GROUPS
POOL_HALO = 16
PEER_HEADS = 8
PEER_NKEYS = 128
PEER_DKEY = 256
PEER_DHALF = PEER_DKEY // 2
PEER_TOPK = 16
PEER_SEL = PEER_HEADS * PEER_TOPK
ROPE_BASE = 10000.0
EPS = 1e-6

Q_DIM = RET_HEADS * RET_DK
V_DIM = RET_HEADS * RET_DV
OFF_Q = 0
OFF_K = OFF_Q + Q_DIM
OFF_V = OFF_K + Q_DIM
OFF_G = OFF_V + V_DIM
OFF_P = OFF_G + V_DIM
OFF_BG = OFF_P + POOL_DIM
IN_COLS = OFF_BG + 2 * D_MODEL

LANES = 128
SUBLANES = 8
ROW_SUB = D_MODEL // LANES
assert ROW_SUB == 2 * SUBLANES
VMEM_LIMIT = 48 * 1024 * 1024

TILE_INPROJ = (1024, 1024)
TILE_MIX = (1024, 512)
TILE_RESID = (1024, 1024)
TILE_POOL = 512
TILE_PEERQ = 512
TILE_FINAL = 512
RET_CHUNKS_PER_STEP = 2


def _cparams(n_axes):
    return pltpu.CompilerParams(
        dimension_semantics=("arbitrary",) * n_axes, vmem_limit_bytes=VMEM_LIMIT)


def _lanes_to_sublanes(x):
    chunks = jnp.stack([x[:, c * LANES:(c + 1) * LANES] for c in range(ROW_SUB)], axis=0)
    return jnp.swapaxes(chunks, 0, 1)


def _sublanes_to_lanes(t):
    chunks = jnp.swapaxes(t, 0, 1)
    return jnp.concatenate([chunks[c] for c in range(ROW_SUB)], axis=1)


def _inproj_kernel(x_ref, nw_ref, w_ref, o_ref, h_scr):
    @pl.when(pl.program_id(1) == 0)
    def _():
        x = x_ref[...]
        ms = jnp.mean(x * x, axis=-1, keepdims=True)
        h_scr[...] = ((x * lax.rsqrt(ms + EPS)) * nw_ref[...]).astype(BF16)

    o_ref[...] = jnp.dot(h_scr[...], w_ref[...], preferred_element_type=F32)


def _inproj(x2, norm_w, w_bf):
    T = x2.shape[0]
    tm, tn = TILE_INPROJ
    return pl.pallas_call(
        _inproj_kernel,
        grid=(T // tm, IN_COLS // tn),
        in_specs=[
            pl.BlockSpec((tm, D_MODEL), lambda i, j: (i, 0)),
            pl.BlockSpec((1, D_MODEL), lambda i, j: (0, 0)),
            pl.BlockSpec((D_MODEL, tn), lambda i, j: (0, j)),
        ],
        out_specs=pl.BlockSpec((tm, tn), lambda i, j: (i, j)),
        out_shape=jax.ShapeDtypeStruct((T, IN_COLS), F32),
        scratch_shapes=[pltpu.VMEM((tm, D_MODEL), BF16)],
        compiler_params=_cparams(2),
        name="inproj",
    )(x2, norm_w.reshape(1, D_MODEL), w_bf)


def _retention_constants(seq):
    H, C = RET_HEADS, RET_CHUNK
    half = RET_DK // 2
    inv_freq = ROPE_BASE ** (-jnp.arange(half, dtype=F32) / half)
    ang = jnp.arange(seq, dtype=F32)[:, None] * inv_freq[None, :]
    cos, sin = jnp.cos(ang), jnp.sin(ang)
    cos_full = jnp.concatenate([cos, cos], axis=-1)
    sin_signed = jnp.concatenate([-sin, sin], axis=-1)
    log_g = jnp.log(1.0 - 2.0 ** (-5.0 - jnp.arange(H, dtype=F32)))
    i = jnp.arange(C, dtype=F32)
    diff = i[:, None] - i[None, :]
    dmask = jnp.where(diff[None] >= 0,
                      jnp.exp(log_g[:, None, None] * jnp.maximum(diff, 0.0)[None]), 0.0)
    xi = jnp.exp(log_g[:, None] * (i[None, :] + 1.0))
    zeta = jnp.exp(log_g[:, None] * (C - 1.0 - i)[None, :])
    g_chunk = jnp.exp(log_g * C)
    xi_b = jnp.broadcast_to(xi[:, :, None], (H, C, RET_DK))
    zeta_b = jnp.broadcast_to(zeta[:, :, None], (H, C, RET_DK))
    g_b = jnp.broadcast_to(g_chunk[:, None, None], (H, SUBLANES, RET_DV))
    return cos_full, sin_signed, dmask, xi_b, zeta_b, g_b


def _ret_kernel(q_ref, k_ref, v_ref, g_ref, cos_ref, sin_ref, dm_ref, xi_ref, ze_ref, gc_ref,
                o_ref, r_scr, *, n_chunks):
    @pl.when(pl.program_id(1) == 0)
    def _():
        r_scr[...] = jnp.zeros_like(r_scr)

    scale = RET_DK ** -0.5
    for c in range(n_chunks):
        rows = slice(c * RET_CHUNK, (c + 1) * RET_CHUNK)
        cos = cos_ref[rows, :]
        sin = sin_ref[rows, :]
        for h in range(RET_HEADS):
            kc = slice(h * RET_DK, (h + 1) * RET_DK)
            vc = slice(h * RET_DV, (h + 1) * RET_DV)
            q = q_ref[0, rows, kc]
            k = k_ref[0, rows, kc]
            qr = q * cos + pltpu.roll(q, RET_DK // 2, 1) * sin
            kr = (k * cos + pltpu.roll(k, RET_DK // 2, 1) * sin) * scale
            vb = v_ref[0, rows, vc].astype(BF16)
            sc = lax.dot_general(qr.astype(BF16), kr.astype(BF16), (((1,), (1,)), ((), ())),
                                 preferred_element_type=F32) * dm_ref[h]
            inner = jnp.dot(sc.astype(BF16), vb, preferred_element_type=F32)
            state = r_scr[h]
            cross = jnp.dot((qr * xi_ref[h]).astype(BF16), state.astype(BF16),
                            preferred_element_type=F32)
            kz = (kr * ze_ref[h]).astype(BF16)
            r_scr[h] = gc_ref[h, 0:1, :] * state + lax.dot_general(
                kz, vb, (((0,), (0,)), ((), ())), preferred_element_type=F32)
            o = inner + cross
            mu = jnp.mean(o, axis=-1, keepdims=True)
            var = jnp.mean((o - mu) ** 2, axis=-1, keepdims=True)
            y = (o - mu) * lax.rsqrt(var + EPS)
            g = g_ref[0, rows, vc]
            o_ref[0, rows, vc] = ((g * jax.nn.sigmoid(g)) * y).astype(BF16)


def _retention(proj3, consts):
    B, S, _ = proj3.shape
    n_chunks = RET_CHUNKS_PER_STEP
    tm = n_chunks * RET_CHUNK
    cos_full, sin_signed, dmask, xi_b, zeta_b, g_b = consts
    H, C = RET_HEADS, RET_CHUNK
    full3 = lambda b, j: (0, 0, 0)
    return pl.pallas_call(
        functools.partial(_ret_kernel, n_chunks=n_chunks),
        grid=(B, S // tm),
        in_specs=[
            pl.BlockSpec((1, tm, Q_DIM), lambda b, j: (b, j, OFF_Q // Q_DIM)),
            pl.BlockSpec((1, tm, Q_DIM), lambda b, j: (b, j, OFF_K // Q_DIM)),
            pl.BlockSpec((1, tm, V_DIM), lambda b, j: (b, j, OFF_V // V_DIM)),
            pl.BlockSpec((1, tm, V_DIM), lambda b, j: (b, j, OFF_G // V_DIM)),
            pl.BlockSpec((tm, RET_DK), lambda b, j: (j, 0)),
            pl.BlockSpec((tm, RET_DK), lambda b, j: (j, 0)),
            pl.BlockSpec((H, C, C), full3),
            pl.BlockSpec((H, C, RET_DK), full3),
            pl.BlockSpec((H, C, RET_DK), full3),
            pl.BlockSpec((H, SUBLANES, RET_DV), full3),
        ],
        out_specs=pl.BlockSpec((1, tm, V_DIM), lambda b, j: (b, j, 0)),
        out_shape=jax.ShapeDtypeStruct((B, S, V_DIM), BF16),
        scratch_shapes=[pltpu.VMEM((H, RET_DK, RET_DV), F32)],
        compiler_params=_cparams(2),
        name="retention",
    )(proj3, proj3, proj3, proj3, cos_full, sin_signed, dmask, xi_b, zeta_b, g_b)


def _pool_kernel(cur_ref, halo_ref, wl_ref, sc_ref, o_ref, *, tm):
    j = pl.program_id(1)
    cur = cur_ref[0]
    halo = jnp.where(j == 0, 0.0, halo_ref[0])
    ext = jnp.concatenate([halo, cur], axis=0)
    pos = j * tm + lax.broadcasted_iota(I32, (tm, 1), 0)
    for g, w in enumerate(POOL_WINDOWS):
        cols = slice(g * POOL_GDIM, (g + 1) * POOL_GDIM)
        s = ext[:, cols]
        sh = 1
        while sh < w:
            s = s + pltpu.roll(s, sh, 0)
            sh *= 2
        win = s[POOL_HALO:, :]
        count = jnp.minimum(pos + 1, w).astype(F32)
        pooled = win / count - cur[:, cols]
        lin = jnp.dot(pooled.astype(BF16), wl_ref[g], preferred_element_type=F32)
        o_ref[0, :, cols] = (lin * sc_ref[:, cols]).astype(BF16)


def _pool(proj3, wl_bf, pool_scale):
    B, S, _ = proj3.shape
    tm = TILE_POOL
    return pl.pallas_call(
        functools.partial(_pool_kernel, tm=tm),
        grid=(B, S // tm),
        in_specs=[
            pl.BlockSpec((1, tm, POOL_DIM), lambda b, j: (b, j, OFF_P // POOL_DIM)),
            pl.BlockSpec((1, POOL_HALO, POOL_DIM),
                         lambda b, j: (b, jnp.maximum(j * (tm // POOL_HALO) - 1, 0), OFF_P // POOL_DIM)),
            pl.BlockSpec((POOL_GROUPS, POOL_GDIM, POOL_GDIM), lambda b, j: (0, 0, 0)),
            pl.BlockSpec((1, POOL_DIM), lambda b, j: (0, 0)),
        ],
        out_specs=pl.BlockSpec((1, tm, POOL_DIM), lambda b, j: (b, j, 0)),
        out_shape=jax.ShapeDtypeStruct((B, S, POOL_DIM), BF16),
        compiler_params=_cparams(2),
        name="pool",
    )(proj3, proj3, wl_bf, pool_scale.reshape(1, POOL_DIM))


def _mix_kernel(ret_ref, wr_ref, pool_ref, wp_ref, gr_ref, gp_ref, o_ref):
    y_ret = jnp.dot(ret_ref[...], wr_ref[...], preferred_element_type=F32)
    y_pool = jnp.dot(pool_ref[...], wp_ref[...], preferred_element_type=F32)
    mixed = jax.nn.sigmoid(gr_ref[...]) * y_ret + jax.nn.sigmoid(gp_ref[...]) * y_pool
    o_ref[...] = mixed.astype(BF16)


def _mix(ret2, wr_bf, pool2, wp_bf, proj2):
    T = ret2.shape[0]
    tm, tn = TILE_MIX
    off_r = OFF_BG // tn
    off_p = (OFF_BG + D_MODEL) // tn
    return pl.pallas_call(
        _mix_kernel,
        grid=(T // tm, D_MODEL // tn),
        in_specs=[
            pl.BlockSpec((tm, V_DIM), lambda i, j: (i, 0)),
            pl.BlockSpec((V_DIM, tn), lambda i, j: (0, j)),
            pl.BlockSpec((tm, POOL_DIM), lambda i, j: (i, 0)),
            pl.BlockSpec((POOL_DIM, tn), lambda i, j: (0, j)),
            pl.BlockSpec((tm, tn), lambda i, j: (i, off_r + j)),
            pl.BlockSpec((tm, tn), lambda i, j: (i, off_p + j)),
        ],
        out_specs=pl.BlockSpec((tm, tn), lambda i, j: (i, j)),
        out_shape=jax.ShapeDtypeStruct((T, D_MODEL), BF16),
        compiler_params=_cparams(2),
        name="mix",
    )(ret2, wr_bf, pool2, wp_bf, proj2, proj2)


def _resid_kernel(m_ref, w_ref, x_ref, o_ref):
    o_ref[...] = x_ref[...] + jnp.dot(m_ref[...], w_ref[...], preferred_element_type=F32)


def _resid(mixed, w_bf, x2):
    T = x2.shape[0]
    tm, tn = TILE_RESID
    return pl.pallas_call(
        _resid_kernel,
        grid=(T // tm, D_MODEL // tn),
        in_specs=[
            pl.BlockSpec((tm, D_MODEL), lambda i, j: (i, 0)),
            pl.BlockSpec((D_MODEL, tn), lambda i, j: (0, j)),
            pl.BlockSpec((tm, tn), lambda i, j: (i, j)),
        ],
        out_specs=pl.BlockSpec((tm, tn), lambda i, j: (i, j)),
        out_shape=jax.ShapeDtypeStruct((T, D_MODEL), F32),
        compiler_params=_cparams(2),
        name="resid",
    )(mixed, w_bf, x2)


class _Extraction:
    def __init__(self, s, ids, none_id):
        self.s, self.ids, self.none_id = s, ids, none_id
        self.vals, self.sel = [], []

    def step(self):
        m = jnp.max(self.s, axis=0, keepdims=True)
        pick = jnp.min(jnp.where(self.s == m, self.ids, self.none_id), axis=0, keepdims=True)
        hit = self.ids == pick
        self.vals.append(m)
        self.sel.append(pick)
        self.s = jnp.where(hit, -jnp.inf, self.s)
        return hit

    def values(self):
        return jnp.concatenate(self.vals, axis=0)

    def picks(self):
        return jnp.concatenate(self.sel, axis=0)


PEERQ_LANES = LANES
PEERQ_HEADS_PER_ITER = 4


def _peerq_pairs(k):
    singles, i = [], 0
    while i < k and k // (i + 1) > 1:
        singles.append((i, 1, k // (i + 1)))
        i += 1
    assert (k - i) % SUBLANES == 0
    groups = []
    for piece in sorted(singles, key=lambda p: -p[2]):
        for g in groups:
            if sum(p[2] for p in g) % SUBLANES and sum(p[2] for p in g) % SUBLANES + piece[2] <= SUBLANES:
                g.append(piece)
                break
        else:
            groups.append([piece])
    groups = [g + [None] * (-sum(p[2] for p in g) % SUBLANES) for g in groups]
    groups += [[(i0, SUBLANES, 1)] for i0 in range(i, k, SUBLANES)]
    return groups


PEERQ_GROUPS = _peerq_pairs(PEER_TOPK)


def _peerq_kernel(x_ref, nw_ref, wq_ref, keys_ref, h_ref, e_ref, g_ref, q_scr, e_scr, g_scr, h_scr,
                  *, tm):
    x = x_ref[...]
    ms = jnp.mean(x * x, axis=-1, keepdims=True)
    hb = ((x * lax.rsqrt(ms + EPS)) * nw_ref[...]).astype(BF16)
    q_scr[...] = jnp.dot(hb, wq_ref[...], preferred_element_type=F32).astype(BF16)
    h_scr[...] = hb.astype(F32)

    def h_tiles(g, carry):
        rows = pl.ds(pl.multiple_of(g * SUBLANES, SUBLANES), SUBLANES)
        h_ref[rows] = _lanes_to_sublanes(h_scr[rows, :])
        return carry

    lax.fori_loop(0, tm // SUBLANES, h_tiles, 0)

    K = PEER_TOPK
    tl = PEERQ_LANES
    iota_k = lax.broadcasted_iota(I32, (PEER_NKEYS, tl), 0).astype(F32)
    pieces = [p for g in PEERQ_GROUPS for p in g]
    no_pair = float(K * K)

    def table(of_piece, pad):
        rows = [jnp.full((1, tl), pad, F32) if p is None else of_piece(*p) for p in pieces]
        return jnp.concatenate(rows, axis=0)

    def flat_of(i0, ni, nj):
        i = i0 + lax.broadcasted_iota(I32, (ni, tl), 0)
        j = lax.broadcasted_iota(I32, (nj, tl), 0)
        return (i * K + j).astype(F32)

    flat = table(flat_of, no_pair)

    def key_stage(h, toks):
        ex = []
        for p in range(2):
            off = pl.multiple_of(h * PEER_DKEY + p * PEER_DHALF, PEER_DHALF)
            qhp = q_scr[toks, pl.ds(off, PEER_DHALF)]
            s = lax.dot_general(keys_ref[p], qhp, (((1,), (1,)), ((), ())),
                                preferred_element_type=F32)
            ex.append(_Extraction(s, iota_k, float(PEER_NKEYS)))
        return ex

    def pair_stage(keys):
        sv = [e.values() for e in keys]
        si = [e.picks() for e in keys]
        cand = table(lambda i0, ni, nj: sv[0][i0:i0 + ni, :] + sv[1][0:nj, :], -jnp.inf)
        cidx = table(lambda i0, ni, nj: si[0][i0:i0 + ni, :] * PEER_NKEYS + si[1][0:nj, :], -1.0)
        return _Extraction(cand, flat, no_pair), cidx

    def heads(hb, carry):
        problems = [(hb * PEERQ_HEADS_PER_ITER + dh, slice(t0, t0 + tl))
                    for dh in range(PEERQ_HEADS_PER_ITER) for t0 in range(0, tm, tl)]
        pairs = None
        for n in range(len(problems) + 1):
            keys = key_stage(*problems[n]) if n < len(problems) else None
            eidx = []
            for _ in range(K):
                if keys is not None:
                    for e in keys:
                        e.step()
                if pairs is not None:
                    hit = pairs[0].step()
                    eidx.append(jnp.max(jnp.where(hit, pairs[1], -1.0), axis=0, keepdims=True))
            if pairs is not None:
                h, toks = problems[n - 1]
                best = pairs[0].values()
                ex = jnp.exp(best - best[0:1, :])
                row = pl.multiple_of(h * K, K)
                e_scr[pl.ds(row, K), toks] = jnp.concatenate(eidx, axis=0)
                g_scr[pl.ds(row, K), toks] = ex / jnp.sum(ex, axis=0, keepdims=True)
            pairs = pair_stage(keys) if keys is not None else None
        return carry

    lax.fori_loop(0, PEER_HEADS // PEERQ_HEADS_PER_ITER, heads, 0)
    e_ref[...] = e_scr[...].T.astype(I32)
    g_ref[...] = g_scr[...].T


def _peerq(x1, norm_w, wq_bf, keys_bf):
    T = x1.shape[0]
    tm = TILE_PEERQ
    return pl.pallas_call(
        functools.partial(_peerq_kernel, tm=tm),
        grid=(T // tm,),
        in_specs=[
            pl.BlockSpec((tm, D_MODEL), lambda i: (i, 0)),
            pl.BlockSpec((1, D_MODEL), lambda i: (0, 0)),
            pl.BlockSpec((D_MODEL, PEER_HEADS * PEER_DKEY), lambda i: (0, 0)),
            pl.BlockSpec((2, PEER_NKEYS, PEER_DHALF), lambda i: (0, 0, 0)),
        ],
        out_specs=[
            pl.BlockSpec((tm, ROW_SUB, LANES), lambda i: (i, 0, 0)),
            pl.BlockSpec((tm, PEER_SEL), lambda i: (i, 0)),
            pl.BlockSpec((tm, PEER_SEL), lambda i: (i, 0)),
        ],
        out_shape=[
            jax.ShapeDtypeStruct((T, ROW_SUB, LANES), F32),
            jax.ShapeDtypeStruct((T, PEER_SEL), I32),
            jax.ShapeDtypeStruct((T, PEER_SEL), F32),
        ],
        scratch_shapes=[pltpu.VMEM((tm, PEER_HEADS * PEER_DKEY), BF16),
                        pltpu.VMEM((PEER_SEL, tm), F32), pltpu.VMEM((PEER_SEL, tm), F32),
                        pltpu.VMEM((tm, D_MODEL), F32)],
        compiler_params=_cparams(1),
        name="peerq",
    )(x1, norm_w.reshape(1, D_MODEL), wq_bf, keys_bf)


PEER_SLOTS = 8
PEER_AHEAD = 6
PEER_TB = 128
PEER_NEXT = 8
PEER_NACC = 4


def _tree_sum(xs):
    while len(xs) > 1:
        xs = [a + b for a, b in zip(xs[0::2], xs[1::2])] + ([xs[-1]] if len(xs) % 2 else [])
    return xs[0]


def _peer_kernel(idx_ref, gate_ref, h_ref, hn_ref, tab_ref, y_ref, *scratch):
    bufs = scratch[:PEER_SLOTS]
    a_scr, sem = scratch[PEER_SLOTS:]

    def issue(tok, slot):
        for r in range(PEER_SEL):
            pltpu.make_async_copy(tab_ref.at[idx_ref[0, tok, r]], bufs[slot].at[r],
                                  sem.at[slot]).start(priority=r % 2)

    def wait(slot):
        pltpu.make_async_copy(tab_ref.at[pl.ds(0, PEER_SEL)], bufs[slot], sem.at[slot]).wait()

    S = SUBLANES
    lane = lax.broadcasted_iota(I32, (S, LANES), 1)
    sub = lax.broadcasted_iota(I32, (S, LANES), 0)

    def weights(tok, a_part):
        a_row = jnp.sum(a_part, axis=0, keepdims=True)
        act = 0.5 * a_row * (1.0 + lax.erf(a_row * np.float32(math.sqrt(0.5))))
        w_row = (act * gate_ref[pl.ds(tok, 1), :]).astype(BF16).astype(F32)
        return jnp.broadcast_to(w_row, (S, LANES))

    def rows(h, u_slot, v_tok, v_slot, a_prev):
        accs = [jnp.zeros((S, LANES), F32) for _ in range(PEER_NACC)]
        if v_tok is not None:
            w8 = weights(v_tok, a_prev)
            ys = [jnp.zeros((ROW_SUB, LANES), F32) for _ in range(PEER_NACC)]
        for r in range(PEER_SEL):
            c = r % PEER_NACC
            p = bufs[u_slot][r, 0:ROW_SUB, :].astype(F32) * h
            s = jnp.sum(p[0:S] + p[S:2 * S], axis=1, keepdims=True)
            accs[c] = jnp.where(lane == r, s, accs[c])
            if v_tok is not None:
                if r % S == 0:
                    w_col = jnp.sum(jnp.where(lane == sub + r, w8, 0.0), axis=1, keepdims=True)
                ys[c] = ys[c] + w_col[r % S:r % S + 1, :] * bufs[v_slot][r, ROW_SUB:2 * ROW_SUB, :].astype(F32)
        if v_tok is not None:
            y_ref[v_tok] = _tree_sum(ys)
        return _tree_sum(accs)

    step_i = pl.program_id(0)
    n_groups = PEER_TB // PEER_SLOTS

    @pl.when(step_i == 0)
    def _():
        for t in range(PEER_AHEAD + 1):
            issue(t, t)
        wait(0)
        a_scr[...] = rows(h_ref[0], 0, None, None, None)

    def group(g, a_prev):
        for k in range(PEER_SLOTS):
            j = g * PEER_SLOTS + k
            wait((k + 1) % PEER_SLOTS)
            issue(j + (PEER_AHEAD + 1), (k + PEER_AHEAD + 1) % PEER_SLOTS)
            if k + 1 < PEER_SLOTS:
                h = h_ref[j + 1]
            else:
                h = jnp.where(g == n_groups - 1, hn_ref[0], h_ref[jnp.minimum(j + 1, PEER_TB - 1)])
            a_prev = rows(h, (k + 1) % PEER_SLOTS, j, k, a_prev)
        return a_prev

    a_scr[...] = lax.fori_loop(0, n_groups, group, a_scr[...])

    @pl.when(step_i == pl.num_programs(0) - 1)
    def _():
        for t in range(PEER_TB + 1, PEER_TB + PEER_AHEAD + 1):
            wait(t % PEER_SLOTS)


def _peer(eidx, gate, h3, table):
    T = eidx.shape[0]
    tb = PEER_TB
    nb = T // tb
    idx3 = eidx.reshape(nb, tb, PEER_SEL)
    idx_ext = jnp.concatenate([idx3, jnp.roll(idx3[:, :PEER_NEXT], -1, axis=0)], axis=1)
    hn_blocks = T // PEER_NEXT
    return pl.pallas_call(
        _peer_kernel,
        grid=(nb,),
        in_specs=[
            pl.BlockSpec((1, tb + PEER_NEXT, PEER_SEL), lambda i: (i, 0, 0), memory_space=pltpu.SMEM),
            pl.BlockSpec((tb, PEER_SEL), lambda i: (i, 0)),
            pl.BlockSpec((tb, ROW_SUB, LANES), lambda i: (i, 0, 0)),
            pl.BlockSpec((PEER_NEXT, ROW_SUB, LANES),
                         lambda i: (jnp.minimum((i + 1) * (tb // PEER_NEXT), hn_blocks - 1), 0, 0)),
            pl.BlockSpec(memory_space=pl.ANY),
        ],
        out_specs=pl.BlockSpec((tb, ROW_SUB, LANES), lambda i: (i, 0, 0)),
        out_shape=jax.ShapeDtypeStruct((T, ROW_SUB, LANES), F32),
        scratch_shapes=(
            [pltpu.VMEM((PEER_SEL, 2 * ROW_SUB, LANES), BF16) for _ in range(PEER_SLOTS)]
            + [pltpu.VMEM((SUBLANES, LANES), F32), pltpu.SemaphoreType.DMA((PEER_SLOTS,))]),
        compiler_params=_cparams(1),
        name="peer",
    )(idx_ext, gate, h3, h3, table)


PACK_TE = 512


def _pack_kernel(u_ref, v_ref, o_ref):
    def group(g, carry):
        rows = pl.ds(pl.multiple_of(g * SUBLANES, SUBLANES), SUBLANES)
        for half, src in enumerate((u_ref, v_ref)):
            tiles = _lanes_to_sublanes(src[rows, :])
            o_ref[rows, half * ROW_SUB:(half + 1) * ROW_SUB, :] = tiles.astype(BF16)
        return carry

    lax.fori_loop(0, PACK_TE // SUBLANES, group, 0)


def _pack_table(u, v):
    E = u.shape[0]
    return pl.pallas_call(
        _pack_kernel,
        grid=(E // PACK_TE,),
        in_specs=[pl.BlockSpec((PACK_TE, D_MODEL), lambda i: (i, 0)),
                  pl.BlockSpec((PACK_TE, D_MODEL), lambda i: (i, 0))],
        out_specs=pl.BlockSpec((PACK_TE, 2 * ROW_SUB, LANES), lambda i: (i, 0, 0)),
        out_shape=jax.ShapeDtypeStruct((E, 2 * ROW_SUB, LANES), BF16),
        compiler_params=_cparams(1),
        name="pack",
    )(u, v)


def _final_kernel(x_ref, y_ref, nw_ref, o_ref, *, tm):
    unroll = 8

    def groups(gi, carry):
        for u in range(unroll):
            rows = pl.ds(pl.multiple_of((gi * unroll + u) * SUBLANES, SUBLANES), SUBLANES)
            x = x_ref[rows, :] + _sublanes_to_lanes(y_ref[rows])
            ms = jnp.mean(x * x, axis=-1, keepdims=True)
            o_ref[rows, :] = (x * lax.rsqrt(ms + EPS)) * nw_ref[...]
        return carry

    lax.fori_loop(0, tm // (SUBLANES * unroll), groups, 0)


def _final(x1, y, norm_w):
    T = x1.shape[0]
    tm = TILE_FINAL
    return pl.pallas_call(
        functools.partial(_final_kernel, tm=tm),
        grid=(T // tm,),
        in_specs=[
            pl.BlockSpec((tm, D_MODEL), lambda i: (i, 0)),
            pl.BlockSpec((tm, ROW_SUB, LANES), lambda i: (i, 0, 0)),
            pl.BlockSpec((1, D_MODEL), lambda i: (0, 0)),
        ],
        out_specs=pl.BlockSpec((tm, D_MODEL), lambda i: (i, 0)),
        out_shape=jax.ShapeDtypeStruct((T, D_MODEL), F32),
        compiler_params=_cparams(1),
        name="final",
    )(x1, y, norm_w.reshape(1, D_MODEL))


def kernel(x, norm1_w, w_in, w_ret_o, w_pool_lin, pool_scale, w_pool_o, w_out,
           norm2_w, peer_w_q, peer_sub_keys, peer_u, peer_v, final_norm_w):
    B, S, D = x.shape
    T = B * S
    assert D == D_MODEL and w_in.shape == (1, D_MODEL, IN_COLS), "one layer of the stated widths"
    x2 = x.reshape(T, D)
    proj = _inproj(x2, norm1_w[0], w_in[0].astype(BF16))
    proj3 = proj.reshape(B, S, IN_COLS)
    ret = _retention(proj3, _retention_constants(S))
    pooled = _pool(proj3, w_pool_lin[0].astype(BF16), pool_scale[0])
    mixed = _mix(ret.reshape(T, V_DIM), w_ret_o[0].astype(BF16),
                 pooled.reshape(T, POOL_DIM), w_pool_o[0].astype(BF16), proj)
    x1 = _resid(mixed, w_out[0].astype(BF16), x2)
    h2, eidx, gate = _peerq(x1, norm2_w[0], peer_w_q[0].astype(BF16),
                            peer_sub_keys[0].astype(BF16))
    table = _pack_table(peer_u[0], peer_v[0])
    y = _peer(eidx, gate, h2, table)
    out = _final(x1, y, final_norm_w)
    return out.reshape(B, S, D)
```

```python
import functools
import math

import numpy as np
import jax
import jax.numpy as jnp
from jax import lax
from jax.experimental import pallas as pl
from jax.experimental.pallas import tpu as pltpu

F32 = jnp.float32
BF16 = jnp.bfloat16
I32 = jnp.int32

D_MODEL = 2048
RET_HEADS = 8
RET_DK = 128
RET_DV = 256
RET_CHUNK = 128
POOL_WINDOWS = (2, 4, 8, 16)
POOL_GROUPS = len(POOL_WINDOWS)
POOL_DIM = 1024
POOL_GDIM = POOL_DIM // POOL_GROUPS
POOL_HALO = 16
PEER_HEADS = 8
PEER_NKEYS = 128
PEER_DKEY = 256
PEER_DHALF = PEER_DKEY // 2
PEER_TOPK = 16
PEER_SEL = PEER_HEADS * PEER_TOPK
ROPE_BASE = 10000.0
EPS = 1e-6

Q_DIM = RET_HEADS * RET_DK
V_DIM = RET_HEADS * RET_DV
OFF_Q = 0
OFF_K = OFF_Q + Q_DIM
OFF_V = OFF_K + Q_DIM
OFF_G = OFF_V + V_DIM
OFF_P = OFF_G + V_DIM
OFF_BG = OFF_P + POOL_DIM
IN_COLS = OFF_BG + 2 * D_MODEL

LANES = 128
SUBLANES = 8
ROW_SUB = D_MODEL // LANES
assert ROW_SUB == 2 * SUBLANES
VMEM_LIMIT = 48 * 1024 * 1024

TILE_INPROJ = (1024, 1024)
TILE_MIX = (1024, 512)
TILE_RESID = (512, 2048)
TILE_POOL = 1024
TILE_PEERQ = 512
TILE_FINAL = 512
RET_CHUNKS_PER_STEP = 2


def _cparams(n_axes):
    return pltpu.CompilerParams(
        dimension_semantics=("arbitrary",) * n_axes, vmem_limit_bytes=VMEM_LIMIT)


def _lanes_to_sublanes(x):
    chunks = jnp.stack([x[:, c * LANES:(c + 1) * LANES] for c in range(ROW_SUB)], axis=0)
    return jnp.swapaxes(chunks, 0, 1)


def _sublanes_to_lanes(t):
    chunks = jnp.swapaxes(t, 0, 1)
    return jnp.concatenate([chunks[c] for c in range(ROW_SUB)], axis=1)


def _inproj_kernel(x_ref, nw_ref, w_ref, o_ref, h_scr):
    @pl.when(pl.program_id(1) == 0)
    def _():
        x = x_ref[...]
        ms = jnp.mean(x * x, axis=-1, keepdims=True)
        h_scr[...] = ((x * lax.rsqrt(ms + EPS)) * nw_ref[...]).astype(BF16)

    o_ref[...] = jnp.dot(h_scr[...], w_ref[...], preferred_element_type=F32)


def _inproj(x2, norm_w, w_bf):
    T = x2.shape[0]
    tm, tn = TILE_INPROJ
    return pl.pallas_call(
        _inproj_kernel,
        grid=(T // tm, IN_COLS // tn),
        in_specs=[
            pl.BlockSpec((tm, D_MODEL), lambda i, j: (i, 0)),
            pl.BlockSpec((1, D_MODEL), lambda i, j: (0, 0)),
            pl.BlockSpec((D_MODEL, tn), lambda i, j: (0, j)),
        ],
        out_specs=pl.BlockSpec((tm, tn), lambda i, j: (i, j)),
        out_shape=jax.ShapeDtypeStruct((T, IN_COLS), F32),
        scratch_shapes=[pltpu.VMEM((tm, D_MODEL), BF16)],
        compiler_params=_cparams(2),
        name="inproj",
    )(x2, norm_w.reshape(1, D_MODEL), w_bf)


def _retention_constants(seq):
    H, C = RET_HEADS, RET_CHUNK
    half = RET_DK // 2
    inv_freq = ROPE_BASE ** (-jnp.arange(half, dtype=F32) / half)
    ang = jnp.arange(seq, dtype=F32)[:, None] * inv_freq[None, :]
    cos, sin = jnp.cos(ang), jnp.sin(ang)
    cos_full = jnp.concatenate([cos, cos], axis=-1)
    sin_signed = jnp.concatenate([-sin, sin], axis=-1)
    log_g = jnp.log(1.0 - 2.0 ** (-5.0 - jnp.arange(H, dtype=F32)))
    i = jnp.arange(C, dtype=F32)
    diff = i[:, None] - i[None, :]
    dmask = jnp.where(diff[None] >= 0,
                      jnp.exp(log_g[:, None, None] * jnp.maximum(diff, 0.0)[None]), 0.0)
    xi = jnp.exp(log_g[:, None] * (i[None, :] + 1.0))
    zeta = jnp.exp(log_g[:, None] * (C - 1.0 - i)[None, :])
    g_chunk = jnp.exp(log_g * C)
    xi_b = jnp.broadcast_to(xi[:, :, None], (H, C, RET_DK))
    zeta_b = jnp.broadcast_to(zeta[:, :, None], (H, C, RET_DK))
    g_b = jnp.broadcast_to(g_chunk[:, None, None], (H, SUBLANES, RET_DV))
    return cos_full, sin_signed, dmask, xi_b, zeta_b, g_b


def _ret_kernel(q_ref, k_ref, v_ref, g_ref, cos_ref, sin_ref, dm_ref, xi_ref, ze_ref, gc_ref,
                o_ref, r_scr, *, n_chunks):
    @pl.when(pl.program_id(1) == 0)
    def _():
        r_scr[...] = jnp.zeros_like(r_scr)

    scale = RET_DK ** -0.5
    for c in range(n_chunks):
        rows = slice(c * RET_CHUNK, (c + 1) * RET_CHUNK)
        cos = cos_ref[rows, :]
        sin = sin_ref[rows, :]
        for h in range(RET_HEADS):
            kc = slice(h * RET_DK, (h + 1) * RET_DK)
            vc = slice(h * RET_DV, (h + 1) * RET_DV)
            q = q_ref[0, rows, kc]
            k = k_ref[0, rows, kc]
            qr = q * cos + pltpu.roll(q, RET_DK // 2, 1) * sin
            kr = (k * cos + pltpu.roll(k, RET_DK // 2, 1) * sin) * scale
            vb = v_ref[0, rows, vc].astype(BF16)
            sc = lax.dot_general(qr.astype(BF16), kr.astype(BF16), (((1,), (1,)), ((), ())),
                                 preferred_element_type=F32) * dm_ref[h]
            inner = jnp.dot(sc.astype(BF16), vb, preferred_element_type=F32)
            state = r_scr[h]
            cross = jnp.dot((qr * xi_ref[h]).astype(BF16), state.astype(BF16),
                            preferred_element_type=F32)
            kz = (kr * ze_ref[h]).astype(BF16)
            r_scr[h] = gc_ref[h, 0:1, :] * state + lax.dot_general(
                kz, vb, (((0,), (0,)), ((), ())), preferred_element_type=F32)
            o = inner + cross
            mu = jnp.mean(o, axis=-1, keepdims=True)
            var = jnp.mean((o - mu) ** 2, axis=-1, keepdims=True)
            y = (o - mu) * lax.rsqrt(var + EPS)
            g = g_ref[0, rows, vc]
            o_ref[0, rows, vc] = ((g * jax.nn.sigmoid(g)) * y).astype(BF16)


def _retention(proj3, consts):
    B, S, _ = proj3.shape
    n_chunks = RET_CHUNKS_PER_STEP
    tm = n_chunks * RET_CHUNK
    cos_full, sin_signed, dmask, xi_b, zeta_b, g_b = consts
    H, C = RET_HEADS, RET_CHUNK
    full3 = lambda b, j: (0, 0, 0)
    return pl.pallas_call(
        functools.partial(_ret_kernel, n_chunks=n_chunks),
        grid=(B, S // tm),
        in_specs=[
            pl.BlockSpec((1, tm, Q_DIM), lambda b, j: (b, j, OFF_Q // Q_DIM)),
            pl.BlockSpec((1, tm, Q_DIM), lambda b, j: (b, j, OFF_K // Q_DIM)),
            pl.BlockSpec((1, tm, V_DIM), lambda b, j: (b, j, OFF_V // V_DIM)),
            pl.BlockSpec((1, tm, V_DIM), lambda b, j: (b, j, OFF_G // V_DIM)),
            pl.BlockSpec((tm, RET_DK), lambda b, j: (j, 0)),
            pl.BlockSpec((tm, RET_DK), lambda b, j: (j, 0)),
            pl.BlockSpec((H, C, C), full3),
            pl.BlockSpec((H, C, RET_DK), full3),
            pl.BlockSpec((H, C, RET_DK), full3),
            pl.BlockSpec((H, SUBLANES, RET_DV), full3),
        ],
        out_specs=pl.BlockSpec((1, tm, V_DIM), lambda b, j: (b, j, 0)),
        out_shape=jax.ShapeDtypeStruct((B, S, V_DIM), BF16),
        scratch_shapes=[pltpu.VMEM((H, RET_DK, RET_DV), F32)],
        compiler_params=_cparams(2),
        name="retention",
    )(proj3, proj3, proj3, proj3, cos_full, sin_signed, dmask, xi_b, zeta_b, g_b)


def _pool_kernel(cur_ref, halo_ref, wl_ref, sc_ref, o_ref, *, tm):
    j = pl.program_id(1)
    cur = cur_ref[0]
    halo = jnp.where(j == 0, 0.0, halo_ref[0])
    ext = jnp.concatenate([halo, cur], axis=0)
    pos = j * tm + lax.broadcasted_iota(I32, (tm, 1), 0)
    for g, w in enumerate(POOL_WINDOWS):
        cols = slice(g * POOL_GDIM, (g + 1) * POOL_GDIM)
        s = ext[:, cols]
        sh = 1
        while sh < w:
            s = s + pltpu.roll(s, sh, 0)
            sh *= 2
        win = s[POOL_HALO:, :]
        count = jnp.minimum(pos + 1, w).astype(F32)
        pooled = win / count - cur[:, cols]
        lin = jnp.dot(pooled.astype(BF16), wl_ref[g], preferred_element_type=F32)
        o_ref[0, :, cols] = (lin * sc_ref[:, cols]).astype(BF16)


def _pool(proj3, wl_bf, pool_scale):
    B, S, _ = proj3.shape
    tm = TILE_POOL
    return pl.pallas_call(
        functools.partial(_pool_kernel, tm=tm),
        grid=(B, S // tm),
        in_specs=[
            pl.BlockSpec((1, tm, POOL_DIM), lambda b, j: (b, j, OFF_P // POOL_DIM)),
            pl.BlockSpec((1, POOL_HALO, POOL_DIM),
                         lambda b, j: (b, jnp.maximum(j * (tm // POOL_HALO) - 1, 0), OFF_P // POOL_DIM)),
            pl.BlockSpec((POOL_GROUPS, POOL_GDIM, POOL_GDIM), lambda b, j: (0, 0, 0)),
            pl.BlockSpec((1, POOL_DIM), lambda b, j: (0, 0)),
        ],
        out_specs=pl.BlockSpec((1, tm, POOL_DIM), lambda b, j: (b, j, 0)),
        out_shape=jax.ShapeDtypeStruct((B, S, POOL_DIM), BF16),
        compiler_params=_cparams(2),
        name="pool",
    )(proj3, proj3, wl_bf, pool_scale.reshape(1, POOL_DIM))


def _mix_kernel(ret_ref, wr_ref, pool_ref, wp_ref, gr_ref, gp_ref, o_ref):
    y_ret = jnp.dot(ret_ref[...], wr_ref[...], preferred_element_type=F32)
    y_pool = jnp.dot(pool_ref[...], wp_ref[...], preferred_element_type=F32)
    mixed = jax.nn.sigmoid(gr_ref[...]) * y_ret + jax.nn.sigmoid(gp_ref[...]) * y_pool
    o_ref[...] = mixed.astype(BF16)


def _mix(ret2, wr_bf, pool2, wp_bf, proj2):
    T = ret2.shape[0]
    tm, tn = TILE_MIX
    off_r = OFF_BG // tn
    off_p = (OFF_BG + D_MODEL) // tn
    return pl.pallas_call(
        _mix_kernel,
        grid=(T // tm, D_MODEL // tn),
        in_specs=[
            pl.BlockSpec((tm, V_DIM), lambda i, j: (i, 0)),
            pl.BlockSpec((V_DIM, tn), lambda i, j: (0, j)),
            pl.BlockSpec((tm, POOL_DIM), lambda i, j: (i, 0)),
            pl.BlockSpec((POOL_DIM, tn), lambda i, j: (0, j)),
            pl.BlockSpec((tm, tn), lambda i, j: (i, off_r + j)),
            pl.BlockSpec((tm, tn), lambda i, j: (i, off_p + j)),
        ],
        out_specs=pl.BlockSpec((tm, tn), lambda i, j: (i, j)),
        out_shape=jax.ShapeDtypeStruct((T, D_MODEL), BF16),
        compiler_params=_cparams(2),
        name="mix",
    )(ret2, wr_bf, pool2, wp_bf, proj2, proj2)


def _resid_kernel(m_ref, w_ref, x_ref, o_ref):
    o_ref[...] = x_ref[...] + jnp.dot(m_ref[...], w_ref[...], preferred_element_type=F32)


def _resid(mixed, w_bf, x2):
    T = x2.shape[0]
    tm, tn = TILE_RESID
    return pl.pallas_call(
        _resid_kernel,
        grid=(T // tm, D_MODEL // tn),
        in_specs=[
            pl.BlockSpec((tm, D_MODEL), lambda i, j: (i, 0)),
            pl.BlockSpec((D_MODEL, tn), lambda i, j: (0, j)),
            pl.BlockSpec((tm, tn), lambda i, j: (i, j)),
        ],
        out_specs=pl.BlockSpec((tm, tn), lambda i, j: (i, j)),
        out_shape=jax.ShapeDtypeStruct((T, D_MODEL), F32),
        compiler_params=_cparams(2),
        name="resid",
    )(mixed, w_bf, x2)


class _Extraction:
    def __init__(self, s, ids, none_id):
        self.s, self.ids, self.none_id = s, ids, none_id
        self.vals, self.sel = [], []

    def step(self):
        m = jnp.max(self.s, axis=0, keepdims=True)
        pick = jnp.min(jnp.where(self.s == m, self.ids, self.none_id), axis=0, keepdims=True)
        hit = self.ids == pick
        self.vals.append(m)
        self.sel.append(pick)
        self.s = jnp.where(hit, -jnp.inf, self.s)
        return hit

    def values(self):
        return jnp.concatenate(self.vals, axis=0)

    def picks(self):
        return jnp.concatenate(self.sel, axis=0)


PEERQ_LANES = LANES
PEERQ_HEADS_PER_ITER = 4


def _peerq_pairs(k):
    singles, i = [], 0
    while i < k and k // (i + 1) > 1:
        singles.append((i, 1, k // (i + 1)))
        i += 1
    assert (k - i) % SUBLANES == 0
    groups = []
    for piece in sorted(singles, key=lambda p: -p[2]):
        for g in groups:
            if sum(p[2] for p in g) % SUBLANES and sum(p[2] for p in g) % SUBLANES + piece[2] <= SUBLANES:
                g.append(piece)
                break
        else:
            groups.append([piece])
    groups = [g + [None] * (-sum(p[2] for p in g) % SUBLANES) for g in groups]
    groups += [[(i0, SUBLANES, 1)] for i0 in range(i, k, SUBLANES)]
    return groups


PEERQ_GROUPS = _peerq_pairs(PEER_TOPK)


def _peerq_kernel(x_ref, nw_ref, wq_ref, keys_ref, h_ref, e_ref, g_ref, q_scr, e_scr, g_scr, h_scr,
                  *, tm):
    x = x_ref[...]
    ms = jnp.mean(x * x, axis=-1, keepdims=True)
    hb = ((x * lax.rsqrt(ms + EPS)) * nw_ref[...]).astype(BF16)
    q_scr[...] = jnp.dot(hb, wq_ref[...], preferred_element_type=F32).astype(BF16)
    h_scr[...] = hb.astype(F32)

    def h_tiles(g, carry):
        rows = pl.ds(pl.multiple_of(g * SUBLANES, SUBLANES), SUBLANES)
        h_ref[rows] = _lanes_to_sublanes(h_scr[rows, :])
        return carry

    lax.fori_loop(0, tm // SUBLANES, h_tiles, 0)

    K = PEER_TOPK
    tl = PEERQ_LANES
    iota_k = lax.broadcasted_iota(I32, (PEER_NKEYS, tl), 0).astype(F32)
    pieces = [p for g in PEERQ_GROUPS for p in g]
    no_pair = float(K * K)

    def table(of_piece, pad):
        rows = [jnp.full((1, tl), pad, F32) if p is None else of_piece(*p) for p in pieces]
        return jnp.concatenate(rows, axis=0)

    def flat_of(i0, ni, nj):
        i = i0 + lax.broadcasted_iota(I32, (ni, tl), 0)
        j = lax.broadcasted_iota(I32, (nj, tl), 0)
        return (i * K + j).astype(F32)

    flat = table(flat_of, no_pair)

    def key_stage(h, toks):
        ex = []
        for p in range(2):
            off = pl.multiple_of(h * PEER_DKEY + p * PEER_DHALF, PEER_DHALF)
            qhp = q_scr[toks, pl.ds(off, PEER_DHALF)]
            s = lax.dot_general(keys_ref[p], qhp, (((1,), (1,)), ((), ())),
                                preferred_element_type=F32)
            ex.append(_Extraction(s, iota_k, float(PEER_NKEYS)))
        return ex

    def pair_stage(keys):
        sv = [e.values() for e in keys]
        si = [e.picks() for e in keys]
        cand = table(lambda i0, ni, nj: sv[0][i0:i0 + ni, :] + sv[1][0:nj, :], -jnp.inf)
        cidx = table(lambda i0, ni, nj: si[0][i0:i0 + ni, :] * PEER_NKEYS + si[1][0:nj, :], -1.0)
        return _Extraction(cand, flat, no_pair), cidx

    def heads(hb, carry):
        problems = [(hb * PEERQ_HEADS_PER_ITER + dh, slice(t0, t0 + tl))
                    for dh in range(PEERQ_HEADS_PER_ITER) for t0 in range(0, tm, tl)]
        pairs = None
        for n in range(len(problems) + 1):
            keys = key_stage(*problems[n]) if n < len(problems) else None
            eidx = []
            for _ in range(K):
                if keys is not None:
                    for e in keys:
                        e.step()
                if pairs is not None:
                    hit = pairs[0].step()
                    eidx.append(jnp.max(jnp.where(hit, pairs[1], -1.0), axis=0, keepdims=True))
            if pairs is not None:
                h, toks = problems[n - 1]
                best = pairs[0].values()
                ex = jnp.exp(best - best[0:1, :])
                row = pl.multiple_of(h * K, K)
                e_scr[pl.ds(row, K), toks] = jnp.concatenate(eidx, axis=0)
                g_scr[pl.ds(row, K), toks] = ex / jnp.sum(ex, axis=0, keepdims=True)
            pairs = pair_stage(keys) if keys is not None else None
        return carry

    lax.fori_loop(0, PEER_HEADS // PEERQ_HEADS_PER_ITER, heads, 0)
    e_ref[...] = e_scr[...].T.astype(I32)
    g_ref[...] = g_scr[...].T


def _peerq(x1, norm_w, wq_bf, keys_bf):
    T = x1.shape[0]
    tm = TILE_PEERQ
    return pl.pallas_call(
        functools.partial(_peerq_kernel, tm=tm),
        grid=(T // tm,),
        in_specs=[
            pl.BlockSpec((tm, D_MODEL), lambda i: (i, 0)),
            pl.BlockSpec((1, D_MODEL), lambda i: (0, 0)),
            pl.BlockSpec((D_MODEL, PEER_HEADS * PEER_DKEY), lambda i: (0, 0)),
            pl.BlockSpec((2, PEER_NKEYS, PEER_DHALF), lambda i: (0, 0, 0)),
        ],
        out_specs=[
            pl.BlockSpec((tm, ROW_SUB, LANES), lambda i: (i, 0, 0)),
            pl.BlockSpec((tm, PEER_SEL), lambda i: (i, 0)),
            pl.BlockSpec((tm, PEER_SEL), lambda i: (i, 0)),
        ],
        out_shape=[
            jax.ShapeDtypeStruct((T, ROW_SUB, LANES), F32),
            jax.ShapeDtypeStruct((T, PEER_SEL), I32),
            jax.ShapeDtypeStruct((T, PEER_SEL), F32),
        ],
        scratch_shapes=[pltpu.VMEM((tm, PEER_HEADS * PEER_DKEY), BF16),
                        pltpu.VMEM((PEER_SEL, tm), F32), pltpu.VMEM((PEER_SEL, tm), F32),
                        pltpu.VMEM((tm, D_MODEL), F32)],
        compiler_params=_cparams(1),
        name="peerq",
    )(x1, norm_w.reshape(1, D_MODEL), wq_bf, keys_bf)


PEER_SLOTS = 8
PEER_AHEAD = 6
PEER_TB = 128
PEER_NEXT = 8
PEER_NACC = 4


def _tree_sum(xs):
    while len(xs) > 1:
        xs = [a + b for a, b in zip(xs[0::2], xs[1::2])] + ([xs[-1]] if len(xs) % 2 else [])
    return xs[0]


def _peer_kernel(idx_ref, gate_ref, h_ref, hn_ref, tab_ref, y_ref, *scratch):
    bufs = scratch[:PEER_SLOTS]
    a_scr, sem = scratch[PEER_SLOTS:]

    def issue(tok, slot):
        for r in range(PEER_SEL):
            pltpu.make_async_copy(tab_ref.at[idx_ref[0, tok, r]], bufs[slot].at[r],
                                  sem.at[slot]).start(priority=r % 2)

    def wait(slot):
        pltpu.make_async_copy(tab_ref.at[pl.ds(0, PEER_SEL)], bufs[slot], sem.at[slot]).wait()

    S = SUBLANES
    lane = lax.broadcasted_iota(I32, (S, LANES), 1)
    sub = lax.broadcasted_iota(I32, (S, LANES), 0)

    def weights(tok, a_part):
        a_row = jnp.sum(a_part, axis=0, keepdims=True)
        act = 0.5 * a_row * (1.0 + lax.erf(a_row * np.float32(math.sqrt(0.5))))
        w_row = (act * gate_ref[pl.ds(tok, 1), :]).astype(BF16).astype(F32)
        return jnp.broadcast_to(w_row, (S, LANES))

    def rows(h, u_slot, v_tok, v_slot, a_prev):
        accs = [jnp.zeros((S, LANES), F32) for _ in range(PEER_NACC)]
        if v_tok is not None:
            w8 = weights(v_tok, a_prev)
            ys = [jnp.zeros((ROW_SUB, LANES), F32) for _ in range(PEER_NACC)]
        for r in range(PEER_SEL):
            c = r % PEER_NACC
            p = bufs[u_slot][r, 0:ROW_SUB, :].astype(F32) * h
            s = jnp.sum(p[0:S] + p[S:2 * S], axis=1, keepdims=True)
            accs[c] = jnp.where(lane == r, s, accs[c])
            if v_tok is not None:
                if r % S == 0:
                    w_col = jnp.sum(jnp.where(lane == sub + r, w8, 0.0), axis=1, keepdims=True)
                ys[c] = ys[c] + w_col[r % S:r % S + 1, :] * bufs[v_slot][r, ROW_SUB:2 * ROW_SUB, :].astype(F32)
        if v_tok is not None:
            y_ref[v_tok] = _tree_sum(ys)
        return _tree_sum(accs)

    step_i = pl.program_id(0)
    n_groups = PEER_TB // PEER_SLOTS

    @pl.when(step_i == 0)
    def _():
        for t in range(PEER_AHEAD + 1):
            issue(t, t)
        wait(0)
        a_scr[...] = rows(h_ref[0], 0, None, None, None)

    def group(g, a_prev):
        for k in range(PEER_SLOTS):
            j = g * PEER_SLOTS + k
            wait((k + 1) % PEER_SLOTS)
            issue(j + (PEER_AHEAD + 1), (k + PEER_AHEAD + 1) % PEER_SLOTS)
            if k + 1 < PEER_SLOTS:
                h = h_ref[j + 1]
            else:
                h = jnp.where(g == n_groups - 1, hn_ref[0], h_ref[jnp.minimum(j + 1, PEER_TB - 1)])
            a_prev = rows(h, (k + 1) % PEER_SLOTS, j, k, a_prev)
        return a_prev

    a_scr[...] = lax.fori_loop(0, n_groups, group, a_scr[...])

    @pl.when(step_i == pl.num_programs(0) - 1)
    def _():
        for t in range(PEER_TB + 1, PEER_TB + PEER_AHEAD + 1):
            wait(t % PEER_SLOTS)


def _peer(eidx, gate, h3, table):
    T = eidx.shape[0]
    tb = PEER_TB
    nb = T // tb
    idx3 = eidx.reshape(nb, tb, PEER_SEL)
    idx_ext = jnp.concatenate([idx3, jnp.roll(idx3[:, :PEER_NEXT], -1, axis=0)], axis=1)
    hn_blocks = T // PEER_NEXT
    return pl.pallas_call(
        _peer_kernel,
        grid=(nb,),
        in_specs=[
            pl.BlockSpec((1, tb + PEER_NEXT, PEER_SEL), lambda i: (i, 0, 0), memory_space=pltpu.SMEM),
            pl.BlockSpec((tb, PEER_SEL), lambda i: (i, 0)),
            pl.BlockSpec((tb, ROW_SUB, LANES), lambda i: (i, 0, 0)),
            pl.BlockSpec((PEER_NEXT, ROW_SUB, LANES),
                         lambda i: (jnp.minimum((i + 1) * (tb // PEER_NEXT), hn_blocks - 1), 0, 0)),
            pl.BlockSpec(memory_space=pl.ANY),
        ],
        out_specs=pl.BlockSpec((tb, ROW_SUB, LANES), lambda i: (i, 0, 0)),
        out_shape=jax.ShapeDtypeStruct((T, ROW_SUB, LANES), F32),
        scratch_shapes=(
            [pltpu.VMEM((PEER_SEL, 2 * ROW_SUB, LANES), BF16) for _ in range(PEER_SLOTS)]
            + [pltpu.VMEM((SUBLANES, LANES), F32), pltpu.SemaphoreType.DMA((PEER_SLOTS,))]),
        compiler_params=_cparams(1),
        name="peer",
    )(idx_ext, gate, h3, h3, table)


PACK_TE = 512


def _pack_kernel(u_ref, v_ref, o_ref):
    def group(g, carry):
        rows = pl.ds(pl.multiple_of(g * SUBLANES, SUBLANES), SUBLANES)
        for half, src in enumerate((u_ref, v_ref)):
            tiles = _lanes_to_sublanes(src[rows, :])
            o_ref[rows, half * ROW_SUB:(half + 1) * ROW_SUB, :] = tiles.astype(BF16)
        return carry

    lax.fori_loop(0, PACK_TE // SUBLANES, group, 0)


def _pack_table(u, v):
    E = u.shape[0]
    return pl.pallas_call(
        _pack_kernel,
        grid=(E // PACK_TE,),
        in_specs=[pl.BlockSpec((PACK_TE, D_MODEL), lambda i: (i, 0)),
                  pl.BlockSpec((PACK_TE, D_MODEL), lambda i: (i, 0))],
        out_specs=pl.BlockSpec((PACK_TE, 2 * ROW_SUB, LANES), lambda i: (i, 0, 0)),
        out_shape=jax.ShapeDtypeStruct((E, 2 * ROW_SUB, LANES), BF16),
        compiler_params=_cparams(1),
        name="pack",
    )(u, v)


def _final_kernel(x_ref, y_ref, nw_ref, o_ref, *, tm):
    unroll = 8

    def groups(gi, carry):
        for u in range(unroll):
            rows = pl.ds(pl.multiple_of((gi * unroll + u) * SUBLANES, SUBLANES), SUBLANES)
            x = x_ref[rows, :] + _sublanes_to_lanes(y_ref[rows])
            ms = jnp.mean(x * x, axis=-1, keepdims=True)
            o_ref[rows, :] = (x * lax.rsqrt(ms + EPS)) * nw_ref[...]
        return carry

    lax.fori_loop(0, tm // (SUBLANES * unroll), groups, 0)


def _final(x1, y, norm_w):
    T = x1.shape[0]
    tm = TILE_FINAL
    return pl.pallas_call(
        functools.partial(_final_kernel, tm=tm),
        grid=(T // tm,),
        in_specs=[
            pl.BlockSpec((tm, D_MODEL), lambda i: (i, 0)),
            pl.BlockSpec((tm, ROW_SUB, LANES), lambda i: (i, 0, 0)),
            pl.BlockSpec((1, D_MODEL), lambda i: (0, 0)),
        ],
        out_specs=pl.BlockSpec((tm, D_MODEL), lambda i: (i, 0)),
        out_shape=jax.ShapeDtypeStruct((T, D_MODEL), F32),
        compiler_params=_cparams(1),
        name="final",
    )(x1, y, norm_w.reshape(1, D_MODEL))


def kernel(x, norm1_w, w_in, w_ret_o, w_pool_lin, pool_scale, w_pool_o, w_out,
           norm2_w, peer_w_q, peer_sub_keys, peer_u, peer_v, final_norm_w):
    B, S, D = x.shape
    T = B * S
    assert D == D_MODEL and w_in.shape == (1, D_MODEL, IN_COLS), "one layer of the stated widths"
    x2 = x.reshape(T, D)
    proj = _inproj(x2, norm1_w[0], w_in[0].astype(BF16))
    proj3 = proj.reshape(B, S, IN_COLS)
    ret = _retention(proj3, _retention_constants(S))
    pooled = _pool(proj3, w_pool_lin[0].astype(BF16), pool_scale[0])
    mixed = _mix(ret.reshape(T, V_DIM), w_ret_o[0].astype(BF16),
                 pooled.reshape(T, POOL_DIM), w_pool_o[0].astype(BF16), proj)
    x1 = _resid(mixed, w_out[0].astype(BF16), x2)
    h2, eidx, gate = _peerq(x1, norm2_w[0], peer_w_q[0].astype(BF16),
                            peer_sub_keys[0].astype(BF16))
    table = _pack_table(peer_u[0], peer_v[0])
    y = _peer(eidx, gate, h2, table)
    out = _final(x1, y, final_norm_w)
    return out.reshape(B, S, D)
```
